```python
import jax, jax.numpy as jnp
from jax import lax
import numpy as np

D_MODEL = 2048
BATCH = 16
SEQ = 2048
DEPTH = 1
DEC_BATCH = 4
DEC_SEQ = 2048
PAST_LEN = 128

MIX_WIDTH = D_MODEL
RWKV_WIDTH = MIX_WIDTH // 2
CONV_WIDTH = MIX_WIDTH - RWKV_WIDTH
HEAD_SIZE = 64
RWKV_HEADS = RWKV_WIDTH // HEAD_SIZE
CONV_TAPS = 3
DECAY_RANK = 64
AAA_RANK = 64
GATE_RANK = 160
RWKV_COLS = 3 * RWKV_WIDTH + 2 * DECAY_RANK + AAA_RANK + GATE_RANK
CONV_COLS = 3 * CONV_WIDTH
IN_COLS = RWKV_COLS + CONV_COLS
D_FF = -(-8 * D_MODEL // (3 * 256)) * 256
N_MOD = 6
RMS_EPS = 1e-6
LNX_EPS = 64e-5

kernel_name = "hymba_rwkv7_shortconv_adaln_encoder"


def rms_norm(x, g):
    xf = x.astype(jnp.float32)
    y = xf * lax.rsqrt(jnp.mean(xf * xf, axis=-1, keepdims=True) + RMS_EPS)
    return y.astype(x.dtype) * g


def bidir_token_shift(h):
    prev = jnp.pad(h[:, :-1], ((0, 0), (1, 0), (0, 0)))
    nxt = jnp.pad(h[:, 1:], ((0, 0), (0, 1), (0, 0)))
    even = (jnp.arange(h.shape[-1]) % 2) == 0
    return jnp.where(even, prev, nxt)


def rwkv7_step(S, inp):
    r_t, w_t, k_t, v_t, kk_t, b_t = inp
    sa = jnp.einsum('dbhij,dbhj->dbhi', S, -kk_t)
    S = S * w_t[..., None, :] + sa[..., :, None] * b_t[..., None, :] + v_t[..., :, None] * k_t[..., None, :]
    y = jnp.einsum('dbhij,dbhj->dbhi', S, r_t)
    return S, y


def rwkv7_time_mix(h_r, h_k, h_v, h_wf, h_wb, h_a, h_g,
                   w0_decay, w2_decay, a0, a2, g2_gate, k_k, k_a, r_k, lnx_g, lnx_b):
    f32 = jnp.float32
    bsz, T, C = h_r.shape
    heads = lambda t: t.reshape(bsz, T, RWKV_HEADS, HEAD_SIZE)
    w_lora = jnp.stack([jnp.tanh(h_wf) @ w2_decay[0], jnp.tanh(h_wb) @ w2_decay[1]])
    w_log = -jax.nn.softplus(-(w0_decay[:, None, None, :] + w_lora).astype(f32)) - 0.5
    decay = jnp.exp(-jnp.exp(w_log)).reshape(2, bsz, T, RWKV_HEADS, HEAD_SIZE)
    a = jax.nn.sigmoid(a0 + h_a @ a2)
    g = jax.nn.sigmoid(h_g) @ g2_gate
    kk = heads(h_k * k_k).astype(f32)
    kk = kk / jnp.maximum(jnp.sqrt(jnp.sum(kk * kk, axis=-1, keepdims=True)), 1e-12)
    k = h_k * (1 + (a - 1) * k_a)
    r = heads(h_r).astype(f32)
    k = heads(k).astype(f32)
    v = heads(h_v).astype(f32)
    a = heads(a).astype(f32)

    def both(t):
        return jnp.moveaxis(jnp.stack([t, t[:, ::-1]]), 2, 0)

    dec_s = jnp.moveaxis(jnp.stack([decay[0], decay[1][:, ::-1]]), 2, 0)
    xs = (both(r), dec_s, both(k), both(v), both(kk), both(kk * a))
    S0 = jnp.zeros((2, bsz, RWKV_HEADS, HEAD_SIZE, HEAD_SIZE), f32)
    _, ys = lax.scan(rwkv7_step, S0, xs)
    y = jnp.moveaxis(ys[:, 0], 0, 1) + jnp.moveaxis(ys[::-1, 1], 0, 1)
    mean = jnp.mean(y, axis=-1, keepdims=True)
    var = jnp.mean(jnp.square(y - mean), axis=-1, keepdims=True)
    y = ((y - mean) * lax.rsqrt(var + LNX_EPS)).reshape(bsz, T, C) * lnx_g + lnx_b
    bonus = jnp.sum(r * k * r_k, axis=-1, keepdims=True) * v
    y = y + bonus.reshape(bsz, T, C)
    return (y * g).astype(h_r.dtype)


def short_conv_mixer(h_b, h_c, h_x, conv_w):
    z = h_c * h_x
    T = z.shape[1]
    zp = jnp.pad(z, ((0, 0), (1, 1), (0, 0)))
    conv = zp[:, :T] * conv_w[0] + zp[:, 1:T + 1] * conv_w[1] + zp[:, 2:] * conv_w[2]
    return h_b * conv


def encoder_layer(x, c, w_ada, b_ada, norm1_g, w_in, mu_shift, w0_decay, w2_decay, a0, a2, g2_gate,
                  k_k, k_a, r_k, lnx_g, lnx_b, conv_w, w_out, norm2_g, w_ffn_gate, w_ffn_up, w_ffn_down):
    mod = jax.nn.silu(c) @ w_ada + b_ada
    sh1, sc1, gt1, sh2, sc2, gt2 = jnp.split(mod[:, None, :], N_MOD, axis=-1)
    hn = rms_norm(x, norm1_g) * (1 + sc1) + sh1
    h = hn @ w_in
    h_rw, h_cv = h[..., :RWKV_COLS], h[..., RWKV_COLS:]
    h_rw = h_rw + mu_shift * (bidir_token_shift(h_rw) - h_rw)
    cuts = [RWKV_WIDTH, 2 * RWKV_WIDTH, 3 * RWKV_WIDTH, 3 * RWKV_WIDTH + DECAY_RANK,
            3 * RWKV_WIDTH + 2 * DECAY_RANK, 3 * RWKV_WIDTH + 2 * DECAY_RANK + AAA_RANK]
    h_r, h_k, h_v, h_wf, h_wb, h_a, h_g = jnp.split(h_rw, cuts, axis=-1)
    h_b, h_c, h_x = jnp.split(h_cv, 3, axis=-1)
    y_rwkv = rwkv7_time_mix(h_r, h_k, h_v, h_wf, h_wb, h_a, h_g, w0_decay, w2_decay, a0, a2, g2_gate,
                            k_k, k_a, r_k, lnx_g, lnx_b)
    y_conv = short_conv_mixer(h_b, h_c, h_x, conv_w)
    x = x + gt1 * (jnp.concatenate([y_rwkv, y_conv], axis=-1) @ w_out)
    hn = rms_norm(x, norm2_g) * (1 + sc2) + sh2
    ff = (jax.nn.silu(hn @ w_ffn_gate) * (hn @ w_ffn_up)) @ w_ffn_down
    return x + gt2 * ff


def setup_inputs(seed: int = 0) -> dict:
    key = jax.random.key(seed)
    ks = jax.random.split(key, 32)
    L, D, f32 = DEPTH, D_MODEL, jnp.float32
    nrm = lambda k, shape, s: jax.random.normal(k, shape, f32) * s
    return {
        "x_prompt": nrm(ks[0], (BATCH, SEQ, D), 1.0),
        "x_sample": nrm(ks[1], (DEC_BATCH, DEC_SEQ, D), 1.0),
        "c_prompt": nrm(ks[2], (BATCH, D), 1.0),
        "c_sample": nrm(ks[3], (DEC_BATCH, D), 1.0),
        "w_ada": nrm(ks[4], (L, D, N_MOD * D), 0.5 * D ** -0.5),
        "b_ada": nrm(ks[5], (L, N_MOD * D), 0.02),
        "norm1_g": 1.0 + nrm(ks[6], (L, D), 0.02),
        "w_in": nrm(ks[7], (L, D, IN_COLS), D ** -0.5),
        "mu_shift": jax.random.uniform(ks[8], (L, RWKV_COLS), f32, 0.0, 1.0),
        "w0_decay": jax.random.uniform(ks[9], (L, 2, RWKV_WIDTH), f32, -5.0, 0.5),
        "w2_decay": nrm(ks[10], (L, 2, DECAY_RANK, RWKV_WIDTH), 0.1 * DECAY_RANK ** -0.5),
        "a0": nrm(ks[11], (L, RWKV_WIDTH), 0.1),
        "a2": nrm(ks[12], (L, AAA_RANK, RWKV_WIDTH), 0.5 * AAA_RANK ** -0.5),
        "g2_gate": nrm(ks[13], (L, GATE_RANK, RWKV_WIDTH), GATE_RANK ** -0.5),
        "k_k": 0.85 + nrm(ks[14], (L, RWKV_WIDTH), 0.02),
        "k_a": 1.0 + nrm(ks[15], (L, RWKV_WIDTH), 0.02),
        "r_k": nrm(ks[16], (L, RWKV_HEADS, HEAD_SIZE), 0.1),
        "lnx_g": 1.0 + nrm(ks[17], (L, RWKV_WIDTH), 0.02),
        "lnx_b": nrm(ks[18], (L, RWKV_WIDTH), 0.02),
        "conv_w": nrm(ks[19], (L, CONV_TAPS, CONV_WIDTH), CONV_TAPS ** -0.5),
        "w_out": nrm(ks[20], (L, MIX_WIDTH, D), MIX_WIDTH ** -0.5),
        "norm2_g": 1.0 + nrm(ks[21], (L, D), 0.02),
        "w_ffn_gate": nrm(ks[22], (L, D, D_FF), D ** -0.5),
        "w_ffn_up": nrm(ks[23], (L, D, D_FF), D ** -0.5),
        "w_ffn_down": nrm(ks[24], (L, D_FF, D), D_FF ** -0.5),
        "norm_f_g": 1.0 + nrm(ks[25], (D,), 0.02),
    }


def reference(x_prompt, x_sample, c_prompt, c_sample, w_ada, b_ada, norm1_g, w_in, mu_shift, w0_decay,
              w2_decay, a0, a2, g2_gate, k_k, k_a, r_k, lnx_g, lnx_b, conv_w, w_out, norm2_g,
              w_ffn_gate, w_ffn_up, w_ffn_down, norm_f_g):
    y_p, y_s = x_prompt, x_sample
    for l in range(DEPTH):
        layer = (w_ada[l], b_ada[l], norm1_g[l], w_in[l], mu_shift[l], w0_decay[l], w2_decay[l], a0[l],
                 a2[l], g2_gate[l], k_k[l], k_a[l], r_k[l], lnx_g[l], lnx_b[l], conv_w[l], w_out[l],
                 norm2_g[l], w_ffn_gate[l], w_ffn_up[l], w_ffn_down[l])
        y_p = encoder_layer(y_p, c_prompt, *layer)
        y_s = encoder_layer(y_s, c_sample, *layer)
    y_prompt = rms_norm(y_p, norm_f_g)
    y_sample = rms_norm(y_s, norm_f_g)
    return (y_prompt, y_sample)
```

```python
import functools
import math

import jax
import jax.numpy as jnp
from jax import lax
from jax.experimental import pallas as pl
from jax.experimental.pallas import tpu as pltpu

F32 = jnp.float32
BF16 = jnp.bfloat16

D_MODEL = 2048
RWKV_WIDTH = 1024
CONV_WIDTH = 1024
HEAD = 64
DECAY_RANK = 64
AAA_RANK = 64
GATE_RANK = 160
SMALL_COLS = 2 * DECAY_RANK + AAA_RANK + GATE_RANK
SMALL_PAD = 512
RKV_COLS = 3 * RWKV_WIDTH
CONV_COLS = 3 * CONV_WIDTH
H_COLS = RKV_COLS + CONV_COLS + SMALL_PAD
D_FF = 5632
RMS_EPS = 1e-6
LNX_EPS = 64e-5

CHUNK = 64
LANES = 128
VMEM_LIMIT = 56 * 1024 * 1024
FFN_DOWN_COLS = 512


def _cparams(n_axes):
    return pltpu.CompilerParams(dimension_semantics=("arbitrary",) * n_axes,
                                vmem_limit_bytes=VMEM_LIMIT)


def _dot(a, b):
    return jnp.dot(a, b, preferred_element_type=F32)


def _dot_nt(a, b):
    return lax.dot_general(a, b, (((1,), (1,)), ((), ())), preferred_element_type=F32)


def _split_bf16(x):
    hi = x.astype(BF16)
    lo = (x - hi.astype(F32)).astype(BF16)
    return hi, lo


def _head_sum_bcast(x, two_pass):
    n = 2 * LANES
    row = lax.broadcasted_iota(jnp.int32, (n, n), 0) >> 6
    col = lax.broadcasted_iota(jnp.int32, (n, n), 1) >> 6
    ones_bd = jnp.where(row == col, 1.0, 0.0).astype(BF16)
    outs = []
    for q in range(x.shape[1] // n):
        xs = x[:, q * n:(q + 1) * n]
        if two_pass:
            hi, lo = _split_bf16(xs)
            outs.append(_dot(hi, ones_bd) + _dot(lo, ones_bd))
        else:
            outs.append(_dot(xs.astype(BF16), ones_bd))
    return jnp.concatenate(outs, axis=1)


def _mod_kernel(c_ref, w_ref, b_ref, o_ref):
    c = c_ref[...]
    s = c * jax.nn.sigmoid(c)
    o_ref[...] = _dot(s.astype(BF16), w_ref[...].astype(BF16)) + b_ref[...]


def _mod(c, w_ada, b_ada):
    nb, d = c.shape
    n = w_ada.shape[1]
    tn = 1024
    return pl.pallas_call(
        _mod_kernel,
        grid=(n // tn,),
        in_specs=[pl.BlockSpec((nb, d), lambda j: (0, 0)),
                  pl.BlockSpec((d, tn), lambda j: (0, j)),
                  pl.BlockSpec((1, tn), lambda j: (0, j))],
        out_specs=pl.BlockSpec((nb, tn), lambda j: (0, j)),
        out_shape=jax.ShapeDtypeStruct((nb, n), F32),
        compiler_params=_cparams(1),
        name="mod",
    )(c, w_ada, b_ada.reshape(1, n))


def _inproj_kernel(x_ref, sc_ref, sh_ref, g_ref, w_ref, o_ref, hn_ref):
    @pl.when(pl.program_id(1) == 0)
    def _():
        x = x_ref[...]
        ms = jnp.mean(x * x, axis=-1, keepdims=True)
        y = x * lax.rsqrt(ms + RMS_EPS) * g_ref[...]
        hn_ref[...] = (y * (1.0 + sc_ref[0]) + sh_ref[0]).astype(BF16)

    o_ref[...] = _dot(hn_ref[...], w_ref[...])


def _in_proj(x2, sc, sh, g1, w_pad, seq):
    m, d = x2.shape
    tm, tn = 1024, 512
    per_seq = seq // tm
    return pl.pallas_call(
        _inproj_kernel,
        grid=(m // tm, H_COLS // tn),
        in_specs=[pl.BlockSpec((tm, d), lambda i, j: (i, 0)),
                  pl.BlockSpec((1, 1, d), lambda i, j: (i // per_seq, 0, 0)),
                  pl.BlockSpec((1, 1, d), lambda i, j: (i // per_seq, 0, 0)),
                  pl.BlockSpec((1, d), lambda i, j: (0, 0)),
                  pl.BlockSpec((d, tn), lambda i, j: (0, j))],
        out_specs=pl.BlockSpec((tm, tn), lambda i, j: (i, j)),
        out_shape=jax.ShapeDtypeStruct((m, H_COLS), F32),
        scratch_shapes=[pltpu.VMEM((tm, d), BF16)],
        compiler_params=_cparams(2),
        name="in_proj",
    )(x2, sc, sh, g1, w_pad)


def _shifted(main, prev8, next8, first, last):
    rows, cols = main.shape
    ridx = lax.broadcasted_iota(jnp.int32, (rows, cols), 0)
    cidx = lax.broadcasted_iota(jnp.int32, (rows, cols), 1)
    prev_row = jnp.where(first, 0.0, prev8[7:8, :])
    next_row = jnp.where(last, 0.0, next8[0:1, :])
    prv = jnp.where(ridx == 0, prev_row, pltpu.roll(main, 1, axis=0))
    nxt = jnp.where(ridx == rows - 1, next_row, pltpu.roll(main, rows - 1, axis=0))
    return jnp.where((cidx & 1) == 0, prv, nxt)


def _prep_kernel(seq,
                 rkv_ref, rkvp_ref, rkvn_ref, sm_ref, smp_ref, smn_ref,
                 mu_rkv_ref, mu_sm_ref, wfb_ref, wa_ref, wg_ref,
                 w0_ref, a0_ref, kk_ref, ka_ref, rk_ref,
                 r_out, k_out, v_out, kk_out, b_out, lwf_out, lwb_out, g_out, bonus_out):
    tp = rkv_ref.shape[0]
    t0 = (pl.program_id(0) * tp) % seq
    first = t0 == 0
    last = t0 + tp == seq

    sm = sm_ref[...]
    xs_sm = sm + mu_sm_ref[...] * (_shifted(sm, smp_ref[...], smn_ref[...], first, last) - sm)
    lora_fb = _dot(jnp.tanh(xs_sm[:, 0:LANES]).astype(BF16), wfb_ref[...])
    lora_a = _dot(xs_sm[:, LANES:2 * LANES].astype(BF16), wa_ref[...])
    g = _dot(jax.nn.sigmoid(xs_sm[:, LANES:]).astype(BF16), wg_ref[...])

    decay_scale = math.exp(-0.5)
    w0 = w0_ref[...]
    lwf_out[...] = -decay_scale * jax.nn.sigmoid(w0[0:1] + lora_fb[:, :RWKV_WIDTH])
    lwb_out[...] = -decay_scale * jax.nn.sigmoid(w0[1:2] + lora_fb[:, RWKV_WIDTH:])
    g_out[...] = g
    a = jax.nn.sigmoid(a0_ref[...] + lora_a)

    def shifted_cols(c0):
        sl = slice(c0, c0 + RWKV_WIDTH)
        main = rkv_ref[:, sl]
        mu = mu_rkv_ref[:, sl]
        return main + mu * (_shifted(main, rkvp_ref[:, sl], rkvn_ref[:, sl], first, last) - main)

    xk = shifted_cols(RWKV_WIDTH)
    kk_raw = xk * kk_ref[...]
    ss = _head_sum_bcast(kk_raw * kk_raw, two_pass=True)
    kk = kk_raw / jnp.maximum(jnp.sqrt(ss), 1e-12)
    k = xk * (1.0 + (a - 1.0) * ka_ref[...])
    kk_out[...] = kk
    b_out[...] = kk * a
    k_out[...] = k
    r = shifted_cols(0)
    r_out[...] = r
    rk_sum = _head_sum_bcast(r * k * rk_ref[...], two_pass=True)
    v = shifted_cols(2 * RWKV_WIDTH)
    v_out[...] = v
    bonus_out[...] = rk_sum * v


def _prep(h, mu_rkv, mu_sm, wfb, wa, wg, w0, a0, k_k, k_a, r_k, seq):
    m = h.shape[0]
    tp = 256
    nblk8 = m // 8
    small_blk = (RKV_COLS + CONV_COLS) // SMALL_PAD
    row = lambda i: (i, 0)
    prev8 = lambda i: (jnp.maximum(i * (tp // 8) - 1, 0), 0)
    next8 = lambda i: (jnp.minimum((i + 1) * (tp // 8), nblk8 - 1), 0)
    const = lambda i: (0, 0)
    out = jax.ShapeDtypeStruct((m, RWKV_WIDTH), F32)
    return pl.pallas_call(
        functools.partial(_prep_kernel, seq),
        grid=(m // tp,),
        in_specs=[pl.BlockSpec((tp, RKV_COLS), row),
                  pl.BlockSpec((8, RKV_COLS), prev8),
                  pl.BlockSpec((8, RKV_COLS), next8),
                  pl.BlockSpec((tp, SMALL_PAD), lambda i: (i, small_blk)),
                  pl.BlockSpec((8, SMALL_PAD), lambda i: (prev8(i)[0], small_blk)),
                  pl.BlockSpec((8, SMALL_PAD), lambda i: (next8(i)[0], small_blk)),
                  pl.BlockSpec((1, RKV_COLS), const),
                  pl.BlockSpec((1, SMALL_PAD), const),
                  pl.BlockSpec(wfb.shape, const),
                  pl.BlockSpec(wa.shape, const),
                  pl.BlockSpec(wg.shape, const),
                  pl.BlockSpec((2, RWKV_WIDTH), const),
                  pl.BlockSpec((1, RWKV_WIDTH), const),
                  pl.BlockSpec((1, RWKV_WIDTH), const),
                  pl.BlockSpec((1, RWKV_WIDTH), const),
                  pl.BlockSpec((1, RWKV_WIDTH), const)],
        out_specs=[pl.BlockSpec((tp, RWKV_WIDTH), row)] * 9,
        out_shape=[out] * 9,
        compiler_params=_cparams(1),
        name="prep",
    )(h, h, h, h, h, h, mu_rkv, mu_sm, wfb, wa, wg, w0, a0, k_k, k_a, r_k)


def _block_diag(x, lane_lo):
    return jnp.concatenate([jnp.where(lane_lo, x, 0.0).astype(BF16),
                            jnp.where(lane_lo, 0.0, x).astype(BF16)], axis=0)


def _chunk_step(r, k, v, kk, b, lw, cum, g_state, masks, bwd):
    tri_strict, tri_incl, eye2, lane_lo, bd_mask = masks
    gq = jnp.exp(cum)
    gk = jnp.exp(-cum)
    gprev = jnp.exp(cum - lw)
    g_all = gq[0:1] if bwd else gq[CHUNK - 1:CHUNK]
    at = -(kk * gprev)
    rt = r * gq
    kt = k * gk
    bt = b * gk

    lhs = jnp.concatenate([at, rt], axis=0).astype(BF16)
    rhs = jnp.concatenate([_block_diag(kt, lane_lo), _block_diag(bt, lane_lo)], axis=0)
    big = _dot_nt(lhs, rhs)
    a_ak = jnp.where(tri_strict, big[0:CHUNK, 0:LANES], 0.0)
    a_ab = jnp.where(tri_strict, big[0:CHUNK, LANES:], 0.0)
    a_rk = jnp.where(tri_incl, big[CHUNK:, 0:LANES], 0.0)
    a_rb = jnp.where(tri_incl, big[CHUNK:, LANES:], 0.0)

    p = a_ab
    q = eye2 + a_ab
    p = _dot(p.astype(BF16), _block_diag(p, lane_lo))
    for _ in range(4):
        w = jnp.concatenate([_block_diag(q, lane_lo), _block_diag(p, lane_lo)], axis=1)
        pq_pp = _dot(p.astype(BF16), w)
        q = q + pq_pp[:, 0:LANES]
        p = pq_pp[:, LANES:]
    q = q + _dot(p.astype(BF16), _block_diag(q, lane_lo))

    v_bd = _block_diag(v, lane_lo)
    xy = _dot(jnp.concatenate([a_ak, a_rk], axis=0).astype(BF16), v_bd)
    x = xy[0:CHUNK]
    y_intra = xy[CHUNK:]
    wu = _dot(q.astype(BF16), jnp.concatenate([_block_diag(at, lane_lo), _block_diag(x, lane_lo)], axis=1))
    wt = wu[:, 0:LANES]
    ut = wu[:, LANES:]

    s1 = _dot_nt(jnp.concatenate([wt, rt], axis=0).astype(BF16), g_state.astype(BF16))
    u = ut + s1[0:CHUNK]
    y = s1[CHUNK:] + y_intra + _dot(a_rb.astype(BF16), _block_diag(u, lane_lo))
    vu_t = jnp.concatenate([v, u], axis=0).T.astype(BF16)
    kb = jnp.concatenate([kt, bt], axis=0).astype(BF16)
    g_new = jnp.where(bd_mask, (g_state + _dot(vu_t, kb)) * g_all, 0.0)
    return y, g_new


def _scan_kernel(n_pairs, n_chunks,
                 rf_ref, kf_ref, vf_ref, kkf_ref, bf_ref, lwf_ref,
                 rb_ref, kb_ref, vb_ref, kkb_ref, bb_ref, lwb_ref,
                 yf_ref, yb_ref, state_ref):
    @pl.when(pl.program_id(2) == 0)
    def _():
        state_ref[...] = jnp.zeros_like(state_ref)

    tb = n_chunks * CHUNK
    ri = lax.broadcasted_iota(jnp.int32, (CHUNK, LANES), 0)
    ci = lax.broadcasted_iota(jnp.int32, (CHUNK, LANES), 1) & (CHUNK - 1)
    lane_lo = lax.broadcasted_iota(jnp.int32, (CHUNK, LANES), 1) < CHUNK
    eye2 = jnp.where(ri == ci, 1.0, 0.0)
    br = lax.broadcasted_iota(jnp.int32, (LANES, LANES), 0) >> 6
    bc = lax.broadcasted_iota(jnp.int32, (LANES, LANES), 1) >> 6
    bd_mask = br == bc
    masks_f = (ci < ri, ci <= ri, eye2, lane_lo, bd_mask)
    masks_b = (ci > ri, ci >= ri, eye2, lane_lo, bd_mask)

    tr = lax.broadcasted_iota(jnp.int32, (tb, tb), 0)
    tc = lax.broadcasted_iota(jnp.int32, (tb, tb), 1)
    same = (tr >> 6) == (tc >> 6)
    tri_f = jnp.where(same & (tc <= tr), 1.0, 0.0).astype(BF16)
    tri_b = jnp.where(same & (tc >= tr), 1.0, 0.0).astype(BF16)

    def cumsum(tri, lw):
        hi, lo = _split_bf16(lw)
        return _dot(tri, hi) + _dot(tri, lo)

    cum_f = cumsum(tri_f, lwf_ref[...])
    cum_b = cumsum(tri_b, lwb_ref[...])

    states = [[state_ref[d, p] for p in range(n_pairs)] for d in range(2)]
    for step in range(n_chunks):
        for p in range(n_pairs):
            cols = slice(p * LANES, (p + 1) * LANES)
            rows = slice(step * CHUNK, (step + 1) * CHUNK)
            y, states[0][p] = _chunk_step(
                rf_ref[rows, cols], kf_ref[rows, cols], vf_ref[rows, cols], kkf_ref[rows, cols],
                bf_ref[rows, cols], lwf_ref[rows, cols], cum_f[rows, cols], states[0][p], masks_f, False)
            yf_ref[rows, cols] = y
            rows = slice((n_chunks - 1 - step) * CHUNK, (n_chunks - step) * CHUNK)
            y, states[1][p] = _chunk_step(
                rb_ref[rows, cols], kb_ref[rows, cols], vb_ref[rows, cols], kkb_ref[rows, cols],
                bb_ref[rows, cols], lwb_ref[rows, cols], cum_b[rows, cols], states[1][p], masks_b, True)
            yb_ref[rows, cols] = y
    for d in range(2):
        for p in range(n_pairs):
            state_ref[d, p] = states[d][p]


def _scan(r, k, v, kk, b, lwf, lwb, seq):
    m = r.shape[0]
    n_pairs, n_chunks = 2, 4
    tb = n_chunks * CHUNK
    width = n_pairs * LANES
    nb = seq // tb
    fwd = lambda s, g, i: (s * nb + i, g)
    bwd = lambda s, g, i: (s * nb + nb - 1 - i, g)
    blk = lambda im: pl.BlockSpec((tb, width), im)
    out = jax.ShapeDtypeStruct((m, RWKV_WIDTH), F32)
    return pl.pallas_call(
        functools.partial(_scan_kernel, n_pairs, n_chunks),
        grid=(m // seq, RWKV_WIDTH // width, nb),
        in_specs=[blk(fwd)] * 6 + [blk(bwd)] * 6,
        out_specs=[blk(fwd), blk(bwd)],
        out_shape=[out, out],
        scratch_shapes=[pltpu.VMEM((2, n_pairs, LANES, LANES), F32)],
        compiler_params=_cparams(3),
        name="scan",
    )(r, k, v, kk, b, lwf, r, k, v, kk, b, lwb)


def _post_kernel(seq,
                 yf_ref, yb_ref, g_ref, bonus_ref, cv_ref, cvp_ref, cvn_ref,
                 lng_ref, lnb_ref, cw_ref, x_ref, gt_ref, w_ref, o_ref, mix_ref):
    @pl.when(pl.program_id(1) == 0)
    def _():
        tm = yf_ref.shape[0]
        t0 = (pl.program_id(0) * tm) % seq
        first = t0 == 0
        last = t0 + tm == seq
        y = yf_ref[...] + yb_ref[...]
        mean = _head_sum_bcast(y, two_pass=True) * (1.0 / HEAD)
        d = y - mean
        var = _head_sum_bcast(d * d, two_pass=True) * (1.0 / HEAD)
        yn = d * lax.rsqrt(var + LNX_EPS) * lng_ref[...] + lnb_ref[...]
        mix_ref[:, 0:RWKV_WIDTH] = ((yn + bonus_ref[...]) * g_ref[...]).astype(BF16)

        c1, c2 = slice(CONV_WIDTH, 2 * CONV_WIDTH), slice(2 * CONV_WIDTH, 3 * CONV_WIDTH)
        z = cv_ref[:, c1] * cv_ref[:, c2]
        zp = jnp.where(first, 0.0, cvp_ref[7:8, c1] * cvp_ref[7:8, c2])
        zn = jnp.where(last, 0.0, cvn_ref[0:1, c1] * cvn_ref[0:1, c2])
        ridx = lax.broadcasted_iota(jnp.int32, z.shape, 0)
        z_prev = jnp.where(ridx == 0, zp, pltpu.roll(z, 1, axis=0))
        z_next = jnp.where(ridx == tm - 1, zn, pltpu.roll(z, tm - 1, axis=0))
        cw = cw_ref[...]
        conv = z_prev * cw[0:1] + z * cw[1:2] + z_next * cw[2:3]
        mix_ref[:, RWKV_WIDTH:] = (cv_ref[:, 0:CONV_WIDTH] * conv).astype(BF16)

    o_ref[...] = x_ref[...] + gt_ref[0] * _dot(mix_ref[...], w_ref[...])


def _post(yf, yb, g, bonus, h, lnx_g, lnx_b, conv_w, x2, gt1, w_out, seq):
    m, d = x2.shape
    tm, tn = 512, 512
    per_seq = seq // tm
    nblk8 = m // 8
    row = lambda i, j: (i, 0)
    const = lambda i, j: (0, 0)
    return pl.pallas_call(
        functools.partial(_post_kernel, seq),
        grid=(m // tm, d // tn),
        in_specs=[pl.BlockSpec((tm, RWKV_WIDTH), row)] * 4 + [
            pl.BlockSpec((tm, CONV_COLS), lambda i, j: (i, 1)),
            pl.BlockSpec((8, CONV_COLS), lambda i, j: (jnp.maximum(i * (tm // 8) - 1, 0), 1)),
            pl.BlockSpec((8, CONV_COLS), lambda i, j: (jnp.minimum((i + 1) * (tm // 8), nblk8 - 1), 1)),
            pl.BlockSpec((1, RWKV_WIDTH), const),
            pl.BlockSpec((1, RWKV_WIDTH), const),
            pl.BlockSpec((3, CONV_WIDTH), const),
            pl.BlockSpec((tm, tn), lambda i, j: (i, j)),
            pl.BlockSpec((1, 1, tn), lambda i, j: (i // per_seq, 0, j)),
            pl.BlockSpec((d, tn), lambda i, j: (0, j))],
        out_specs=pl.BlockSpec((tm, tn), lambda i, j: (i, j)),
        out_shape=jax.ShapeDtypeStruct((m, d), F32),
        scratch_shapes=[pltpu.VMEM((tm, d), BF16)],
        compiler_params=_cparams(2),
        name="post",
    )(yf, yb, g, bonus, h, h, h, lnx_g, lnx_b, conv_w, x2, gt1, w_out)


def _ffn_kernel(x_ref, sc_ref, sh_ref, gt_ref, g2_ref, gf_ref, wg_ref, wu_ref, wd_ref, o_ref, hn_ref):
    j = pl.program_id(1)

    @pl.when(j == 0)
    def _():
        x = x_ref[...]
        ms = jnp.mean(x * x, axis=-1, keepdims=True)
        y = x * lax.rsqrt(ms + RMS_EPS) * g2_ref[...]
        hn_ref[...] = (y * (1.0 + sc_ref[0]) + sh_ref[0]).astype(BF16)

    hn = hn_ref[...]
    gate = _dot(hn, wg_ref[...])
    up = _dot(hn, wu_ref[...])
    act = (gate * jax.nn.sigmoid(gate) * up).astype(BF16)

    @pl.when(j == 0)
    def _():
        o_ref[...] = jnp.zeros_like(o_ref)

    n_out = o_ref.shape[1]
    for c0 in range(0, n_out, FFN_DOWN_COLS):
        cols = slice(c0, c0 + FFN_DOWN_COLS)
        o_ref[:, cols] += _dot(act, wd_ref[:, cols])

    @pl.when(j == pl.num_programs(1) - 1)
    def _():
        x = x_ref[...] + gt_ref[0] * o_ref[...]
        ms = jnp.mean(x * x, axis=-1, keepdims=True)
        o_ref[...] = x * lax.rsqrt(ms + RMS_EPS) * gf_ref[...]


def _ffn(x1, sc, sh, gt, g2, gf, wg, wu, wd, seq):
    m, d = x1.shape
    tm, tf = 1024, 512
    per_seq = seq // tm
    row = lambda i, j: (i, 0)
    mod = lambda i, j: (i // per_seq, 0, 0)
    const = lambda i, j: (0, 0)
    return pl.pallas_call(
        _ffn_kernel,
        grid=(m // tm, D_FF // tf),
        in_specs=[pl.BlockSpec((tm, d), row, pipeline_mode=pl.Buffered(1)),
                  pl.BlockSpec((1, 1, d), mod),
                  pl.BlockSpec((1, 1, d), mod),
                  pl.BlockSpec((1, 1, d), mod),
                  pl.BlockSpec((1, d), const),
                  pl.BlockSpec((1, d), const),
                  pl.BlockSpec((d, tf), lambda i, j: (0, j)),
                  pl.BlockSpec((d, tf), lambda i, j: (0, j)),
                  pl.BlockSpec((tf, d), lambda i, j: (j, 0))],
        out_specs=pl.BlockSpec((tm, d), row),
        out_shape=jax.ShapeDtypeStruct((m, d), F32),
        scratch_shapes=[pltpu.VMEM((tm, d), BF16)],
        compiler_params=_cparams(2),
        name="ffn",
    )(x1, sc, sh, gt, g2, gf, wg, wu, wd)


def _layer_weights(w_in, mu_shift, w2_decay, a2, g2_gate):
    d = w_in.shape[0]
    pad = H_COLS - (RKV_COLS + CONV_COLS + SMALL_COLS)
    w_pad = jnp.concatenate([w_in[:, :RKV_COLS], w_in[:, RKV_COLS + SMALL_COLS:],
                             w_in[:, RKV_COLS:RKV_COLS + SMALL_COLS], jnp.zeros((d, pad), F32)],
                            axis=1).astype(BF16)
    mu_rkv = mu_shift[:RKV_COLS].reshape(1, RKV_COLS)
    mu_sm = jnp.concatenate([mu_shift[RKV_COLS:], jnp.zeros((pad,), F32)]).reshape(1, SMALL_PAD)
    zeros = lambda rows, cols: jnp.zeros((rows, cols), F32)
    wfb = jnp.concatenate([
        jnp.concatenate([w2_decay[0], zeros(DECAY_RANK, RWKV_WIDTH)], axis=1),
        jnp.concatenate([zeros(DECAY_RANK, RWKV_WIDTH), w2_decay[1]], axis=1)], axis=0).astype(BF16)
    wa = jnp.concatenate([a2, zeros(LANES - AAA_RANK, RWKV_WIDTH)], axis=0).astype(BF16)
    g_rows = SMALL_PAD - LANES
    wg = jnp.concatenate([zeros(AAA_RANK, RWKV_WIDTH), g2_gate,
                          zeros(g_rows - AAA_RANK - GATE_RANK, RWKV_WIDTH)], axis=0).astype(BF16)
    return w_pad, mu_rkv, mu_sm, wfb, wa, wg


def _group(x, mod, lw):
    nb, seq, d = x.shape
    x2 = x.reshape(nb * seq, d)
    sh1, sc1, gt1, sh2, sc2, gt2 = [mod[:, q * d:(q + 1) * d].reshape(nb, 1, d) for q in range(6)]
    h = _in_proj(x2, sc1, sh1, lw["norm1_g"], lw["w_pad"], seq)
    r, k, v, kk, b, lwf, lwb, g, bonus = _prep(
        h, lw["mu_rkv"], lw["mu_sm"], lw["wfb"], lw["wa"], lw["wg"], lw["w0"], lw["a0"],
        lw["k_k"], lw["k_a"], lw["r_k"], seq)
    yf, yb = _scan(r, k, v, kk, b, lwf, lwb, seq)
    x1 = _post(yf, yb, g, bonus, h, lw["lnx_g"], lw["lnx_b"], lw["conv_w"], x2, gt1, lw["w_out"], seq)
    out = _ffn(x1, sc2, sh2, gt2, lw["norm2_g"], lw["norm_f_g"], lw["wg_ffn"], lw["wu_ffn"], lw["wd_ffn"], seq)
    return out.reshape(nb, seq, d)


def kernel(x_prompt, x_sample, c_prompt, c_sample, w_ada, b_ada, norm1_g, w_in, mu_shift, w0_decay, w2_decay, a0, a2, g2_gate, k_k, k_a, r_k, lnx_g, lnx_b, conv_w, w_out, norm2_g, w_ffn_gate, w_ffn_up, w_ffn_down, norm_f_g):
    assert w_ada.shape[0] == 1, "single-layer block"
    l = 0
    row = lambda t: t.reshape(1, -1)
    w_pad, mu_rkv, mu_sm, wfb, wa, wg = _layer_weights(w_in[l], mu_shift[l], w2_decay[l], a2[l], g2_gate[l])
    lw = dict(w_pad=w_pad, mu_rkv=mu_rkv, mu_sm=mu_sm, wfb=wfb, wa=wa, wg=wg,
              norm1_g=row(norm1_g[l]), w0=w0_decay[l], a0=row(a0[l]), k_k=row(k_k[l]), k_a=row(k_a[l]),
              r_k=row(r_k[l]), lnx_g=row(lnx_g[l]), lnx_b=row(lnx_b[l]), conv_w=conv_w[l],
              w_out=w_out[l].astype(BF16), norm2_g=row(norm2_g[l]), norm_f_g=row(norm_f_g),
              wg_ffn=w_ffn_gate[l].astype(BF16), wu_ffn=w_ffn_up[l].astype(BF16),
              wd_ffn=w_ffn_down[l].astype(BF16))
    n_prompt = c_prompt.shape[0]
    mod = _mod(jnp.concatenate([c_prompt, c_sample], axis=0), w_ada[l], b_ada[l])
    y_prompt = _group(x_prompt, mod[:n_prompt], lw)
    y_sample = _group(x_sample, mod[n_prompt:], lw)
    return (y_prompt, y_sample)
```

```python
import functools
import math

import jax
import jax.numpy as jnp
from jax import lax
from jax.experimental import pallas as pl
from jax.experimental.pallas import tpu as pltpu

F32 = jnp.float32
BF16 = jnp.bfloat16

D_MODEL = 2048
RWKV_WIDTH = 1024
CONV_WIDTH = 1024
HEAD = 64
DECAY_RANK = 64
AAA_RANK = 64
GATE_RANK = 160
SMALL_COLS = 2 * DECAY_RANK + AAA_RANK + GATE_RANK
SMALL_PAD = 512
RKV_COLS = 3 * RWKV_WIDTH
CONV_COLS = 3 * CONV_WIDTH
H_COLS = RKV_COLS + CONV_COLS + SMALL_PAD
D_FF = 5632
RMS_EPS = 1e-6
LNX_EPS = 64e-5

CHUNK = 64
LANES = 128
VMEM_LIMIT = 56 * 1024 * 1024
FFN_DOWN_COLS = 512


def _cparams(n_axes):
    return pltpu.CompilerParams(dimension_semantics=("arbitrary",) * n_axes,
                                vmem_limit_bytes=VMEM_LIMIT)


def _dot(a, b):
    return jnp.dot(a, b, preferred_element_type=F32)


def _dot_nt(a, b):
    return lax.dot_general(a, b, (((1,), (1,)), ((), ())), preferred_element_type=F32)


def _split_bf16(x):
    hi = x.astype(BF16)
    lo = (x - hi.astype(F32)).astype(BF16)
    return hi, lo


def _head_sum_bcast(x, two_pass):
    n = 2 * LANES
    row = lax.broadcasted_iota(jnp.int32, (n, n), 0) >> 6
    col = lax.broadcasted_iota(jnp.int32, (n, n), 1) >> 6
    ones_bd = jnp.where(row == col, 1.0, 0.0).astype(BF16)
    outs = []
    for q in range(x.shape[1] // n):
        xs = x[:, q * n:(q + 1) * n]
        if two_pass:
            hi, lo = _split_bf16(xs)
            outs.append(_dot(hi, ones_bd) + _dot(lo, ones_bd))
        else:
            outs.append(_dot(xs.astype(BF16), ones_bd))
    return jnp.concatenate(outs, axis=1)


def _mod_kernel(c_ref, w_ref, b_ref, o_ref):
    c = c_ref[...]
    s = c * jax.nn.sigmoid(c)
    o_ref[...] = _dot(s.astype(BF16), w_ref[...].astype(BF16)) + b_ref[...]


def _mod(c, w_ada, b_ada):
    nb, d = c.shape
    n = w_ada.shape[1]
    tn = 1024
    return pl.pallas_call(
        _mod_kernel,
        grid=(n // tn,),
        in_specs=[pl.BlockSpec((nb, d), lambda j: (0, 0)),
                  pl.BlockSpec((d, tn), lambda j: (0, j)),
                  pl.BlockSpec((1, tn), lambda j: (0, j))],
        out_specs=pl.BlockSpec((nb, tn), lambda j: (0, j)),
        out_shape=jax.ShapeDtypeStruct((nb, n), F32),
        compiler_params=_cparams(1),
        name="mod",
    )(c, w_ada, b_ada.reshape(1, n))


def _inproj_kernel(x_ref, sc_ref, sh_ref, g_ref, w_ref, o_ref, hn_ref):
    @pl.when(pl.program_id(1) == 0)
    def _():
        x = x_ref[...]
        ms = jnp.mean(x * x, axis=-1, keepdims=True)
        y = x * lax.rsqrt(ms + RMS_EPS) * g_ref[...]
        hn_ref[...] = (y * (1.0 + sc_ref[0]) + sh_ref[0]).astype(BF16)

    o_ref[...] = _dot(hn_ref[...], w_ref[...])


def _in_proj(x2, sc, sh, g1, w_pad, seq):
    m, d = x2.shape
    tm, tn = 1024, 512
    per_seq = seq // tm
    return pl.pallas_call(
        _inproj_kernel,
        grid=(m // tm, H_COLS // tn),
        in_specs=[pl.BlockSpec((tm, d), lambda i, j: (i, 0)),
                  pl.BlockSpec((1, 1, d), lambda i, j: (i // per_seq, 0, 0)),
                  pl.BlockSpec((1, 1, d), lambda i, j: (i // per_seq, 0, 0)),
                  pl.BlockSpec((1, d), lambda i, j: (0, 0)),
                  pl.BlockSpec((d, tn), lambda i, j: (0, j))],
        out_specs=pl.BlockSpec((tm, tn), lambda i, j: (i, j)),
        out_shape=jax.ShapeDtypeStruct((m, H_COLS), F32),
        scratch_shapes=[pltpu.VMEM((tm, d), BF16)],
        compiler_params=_cparams(2),
        name="in_proj",
    )(x2, sc, sh, g1, w_pad)


def _shifted(main, prev8, next8, first, last):
    rows, cols = main.shape
    ridx = lax.broadcasted_iota(jnp.int32, (rows, cols), 0)
    cidx = lax.broadcasted_iota(jnp.int32, (rows, cols), 1)
    prev_row = jnp.where(first, 0.0, prev8[7:8, :])
    next_row = jnp.where(last, 0.0, next8[0:1, :])
    prv = jnp.where(ridx == 0, prev_row, pltpu.roll(main, 1, axis=0))
    nxt = jnp.where(ridx == rows - 1, next_row, pltpu.roll(main, rows - 1, axis=0))
    return jnp.where((cidx & 1) == 0, prv, nxt)


def _prep_kernel(seq,
                 rkv_ref, rkvp_ref, rkvn_ref, sm_ref, smp_ref, smn_ref,
                 mu_rkv_ref, mu_sm_ref, wfb_ref, wa_ref, wg_ref,
                 w0_ref, a0_ref, kk_ref, ka_ref, rk_ref,
                 r_out, k_out, v_out, kk_out, b_out, lwf_out, lwb_out, g_out, bonus_out):
    tp = rkv_ref.shape[0]
    t0 = (pl.program_id(0) * tp) % seq
    first = t0 == 0
    last = t0 + tp == seq

    sm = sm_ref[...]
    xs_sm = sm + mu_sm_ref[...] * (_shifted(sm, smp_ref[...], smn_ref[...], first, last) - sm)
    lora_fb = _dot(jnp.tanh(xs_sm[:, 0:LANES]).astype(BF16), wfb_ref[...])
    lora_a = _dot(xs_sm[:, LANES:2 * LANES].astype(BF16), wa_ref[...])
    g = _dot(jax.nn.sigmoid(xs_sm[:, LANES:]).astype(BF16), wg_ref[...])

    decay_scale = math.exp(-0.5)
    w0 = w0_ref[...]
    lwf_out[...] = -decay_scale * jax.nn.sigmoid(w0[0:1] + lora_fb[:, :RWKV_WIDTH])
    lwb_out[...] = -decay_scale * jax.nn.sigmoid(w0[1:2] + lora_fb[:, RWKV_WIDTH:])
    g_out[...] = g
    a = jax.nn.sigmoid(a0_ref[...] + lora_a)

    def shifted_cols(c0):
        sl = slice(c0, c0 + RWKV_WIDTH)
        main = rkv_ref[:, sl]
        mu = mu_rkv_ref[:, sl]
        return main + mu * (_shifted(main, rkvp_ref[:, sl], rkvn_ref[:, sl], first, last) - main)

    xk = shifted_cols(RWKV_WIDTH)
    kk_raw = xk * kk_ref[...]
    ss = _head_sum_bcast(kk_raw * kk_raw, two_pass=True)
    kk = kk_raw / jnp.maximum(jnp.sqrt(ss), 1e-12)
    k = xk * (1.0 + (a - 1.0) * ka_ref[...])
    kk_out[...] = kk
    b_out[...] = kk * a
    k_out[...] = k
    r = shifted_cols(0)
    r_out[...] = r
    rk_sum = _head_sum_bcast(r * k * rk_ref[...], two_pass=True)
    v = shifted_cols(2 * RWKV_WIDTH)
    v_out[...] = v
    bonus_out[...] = rk_sum * v


def _prep(h, mu_rkv, mu_sm, wfb, wa, wg, w0, a0, k_k, k_a, r_k, seq):
    m = h.shape[0]
    tp = 256
    nblk8 = m // 8
    small_blk = (RKV_COLS + CONV_COLS) // SMALL_PAD
    row = lambda i: (i, 0)
    prev8 = lambda i: (jnp.maximum(i * (tp // 8) - 1, 0), 0)
    next8 = lambda i: (jnp.minimum((i + 1) * (tp // 8), nblk8 - 1), 0)
    const = lambda i: (0, 0)
    out = jax.ShapeDtypeStruct((m, RWKV_WIDTH), F32)
    return pl.pallas_call(
        functools.partial(_prep_kernel, seq),
        grid=(m // tp,),
        in_specs=[pl.BlockSpec((tp, RKV_COLS), row),
                  pl.BlockSpec((8, RKV_COLS), prev8),
                  pl.BlockSpec((8, RKV_COLS), next8),
                  pl.BlockSpec((tp, SMALL_PAD), lambda i: (i, small_blk)),
                  pl.BlockSpec((8, SMALL_PAD), lambda i: (prev8(i)[0], small_blk)),
                  pl.BlockSpec((8, SMALL_PAD), lambda i: (next8(i)[0], small_blk)),
                  pl.BlockSpec((1, RKV_COLS), const),
                  pl.BlockSpec((1, SMALL_PAD), const),
                  pl.BlockSpec(wfb.shape, const),
                  pl.BlockSpec(wa.shape, const),
                  pl.BlockSpec(wg.shape, const),
                  pl.BlockSpec((2, RWKV_WIDTH), const),
                  pl.BlockSpec((1, RWKV_WIDTH), const),
                  pl.BlockSpec((1, RWKV_WIDTH), const),
                  pl.BlockSpec((1, RWKV_WIDTH), const),
                  pl.BlockSpec((1, RWKV_WIDTH), const)],
        out_specs=[pl.BlockSpec((tp, RWKV_WIDTH), row)] * 9,
        out_shape=[out] * 9,
        compiler_params=_cparams(1),
        name="prep",
    )(h, h, h, h, h, h, mu_rkv, mu_sm, wfb, wa, wg, w0, a0, k_k, k_a, r_k)


def _block_diag(x, lane_lo):
    return jnp.concatenate([jnp.where(lane_lo, x, 0.0).astype(BF16),
                            jnp.where(lane_lo, 0.0, x).astype(BF16)], axis=0)


def _chunk_step(chains, common):
    eye2, lane_lo, bd_mask = common
    bd = lambda x: _block_diag(x, lane_lo)
    n = len(chains)
    at, rt, kt, bt, g_all = [], [], [], [], []
    for (r, k, v, kk, b, lw, cum, _, _, bwd) in chains:
        gq = jnp.exp(cum)
        gk = jnp.exp(-cum)
        g_all.append(gq[0:1] if bwd else gq[CHUNK - 1:CHUNK])
        at.append(-(kk * jnp.exp(cum - lw)))
        rt.append(r * gq)
        kt.append(k * gk)
        bt.append(b * gk)

    a_ak, a_ab, a_rk, a_rb = [], [], [], []
    for c in range(n):
        tri_strict, tri_incl = chains[c][8]
        lhs = jnp.concatenate([at[c], rt[c]], axis=0).astype(BF16)
        rhs = jnp.concatenate([bd(kt[c]), bd(bt[c])], axis=0)
        big = _dot_nt(lhs, rhs)
        a_ak.append(jnp.where(tri_strict, big[0:CHUNK, 0:LANES], 0.0))
        a_ab.append(jnp.where(tri_strict, big[0:CHUNK, LANES:], 0.0))
        a_rk.append(jnp.where(tri_incl, big[CHUNK:, 0:LANES], 0.0))
        a_rb.append(jnp.where(tri_incl, big[CHUNK:, LANES:], 0.0))

    q = [eye2 + a for a in a_ab]
    p = [_dot(a.astype(BF16), bd(a)) for a in a_ab]
    xy = [_dot(jnp.concatenate([a_ak[c], a_rk[c]], axis=0).astype(BF16), bd(chains[c][2])) for c in range(n)]
    for _ in range(4):
        pq_pp = [_dot(p[c].astype(BF16), jnp.concatenate([bd(q[c]), bd(p[c])], axis=1)) for c in range(n)]
        q = [q[c] + pq_pp[c][:, 0:LANES] for c in range(n)]
        p = [pq_pp[c][:, LANES:] for c in range(n)]
    q = [q[c] + _dot(p[c].astype(BF16), bd(q[c])) for c in range(n)]

    wu = [_dot(q[c].astype(BF16), jnp.concatenate([bd(at[c]), bd(xy[c][0:CHUNK])], axis=1)) for c in range(n)]
    s1 = [_dot_nt(jnp.concatenate([wu[c][:, 0:LANES], rt[c]], axis=0).astype(BF16), chains[c][7].astype(BF16))
          for c in range(n)]
    u = [wu[c][:, LANES:] + s1[c][0:CHUNK] for c in range(n)]
    ys = [s1[c][CHUNK:] + xy[c][CHUNK:] + _dot(a_rb[c].astype(BF16), bd(u[c])) for c in range(n)]
    states = []
    for c in range(n):
        vu_t = jnp.concatenate([chains[c][2], u[c]], axis=0).T.astype(BF16)
        kb = jnp.concatenate([kt[c], bt[c]], axis=0).astype(BF16)
        states.append(jnp.where(bd_mask, (chains[c][7] + _dot(vu_t, kb)) * g_all[c], 0.0))
    return ys, states


def _scan_kernel(n_chunks,
                 rf_ref, kf_ref, vf_ref, kkf_ref, bf_ref, lwf_ref,
                 rb_ref, kb_ref, vb_ref, kkb_ref, bb_ref, lwb_ref,
                 yf_ref, yb_ref, state_ref, cumf_ref, cumb_ref):
    @pl.when(pl.program_id(1) == 0)
    def _():
        state_ref[...] = jnp.zeros_like(state_ref)

    n_pairs = rf_ref.shape[1] // LANES
    tb = n_chunks * CHUNK
    ri = lax.broadcasted_iota(jnp.int32, (CHUNK, LANES), 0)
    ci = lax.broadcasted_iota(jnp.int32, (CHUNK, LANES), 1) & (CHUNK - 1)
    lane_lo = lax.broadcasted_iota(jnp.int32, (CHUNK, LANES), 1) < CHUNK
    eye2 = jnp.where(ri == ci, 1.0, 0.0)
    br = lax.broadcasted_iota(jnp.int32, (LANES, LANES), 0) >> 6
    bc = lax.broadcasted_iota(jnp.int32, (LANES, LANES), 1) >> 6
    common = (eye2, lane_lo, br == bc)
    tri_f = (ci < ri, ci <= ri)
    tri_b = (ci > ri, ci >= ri)

    tr = lax.broadcasted_iota(jnp.int32, (tb, tb), 0)
    tc = lax.broadcasted_iota(jnp.int32, (tb, tb), 1)
    same = (tr >> 6) == (tc >> 6)
    ones_f = jnp.where(same & (tc <= tr), 1.0, 0.0).astype(BF16)
    ones_b = jnp.where(same & (tc >= tr), 1.0, 0.0).astype(BF16)

    def cumsum(ones, lw):
        hi, lo = _split_bf16(lw)
        return _dot(ones, hi) + _dot(ones, lo)

    cumf_ref[...] = cumsum(ones_f, lwf_ref[...])
    cumb_ref[...] = cumsum(ones_b, lwb_ref[...])

    for step in range(n_chunks):
        rows_f = slice(step * CHUNK, (step + 1) * CHUNK)
        rows_b = slice((n_chunks - 1 - step) * CHUNK, (n_chunks - step) * CHUNK)
        chains = []
        for p in range(n_pairs):
            cols = slice(p * LANES, (p + 1) * LANES)
            chains.append((rf_ref[rows_f, cols], kf_ref[rows_f, cols], vf_ref[rows_f, cols],
                           kkf_ref[rows_f, cols], bf_ref[rows_f, cols], lwf_ref[rows_f, cols],
                           cumf_ref[rows_f, cols], state_ref[0, p], tri_f, False))
            chains.append((rb_ref[rows_b, cols], kb_ref[rows_b, cols], vb_ref[rows_b, cols],
                           kkb_ref[rows_b, cols], bb_ref[rows_b, cols], lwb_ref[rows_b, cols],
                           cumb_ref[rows_b, cols], state_ref[1, p], tri_b, True))
        ys, states = _chunk_step(chains, common)
        for p in range(n_pairs):
            cols = slice(p * LANES, (p + 1) * LANES)
            yf_ref[rows_f, cols] = ys[2 * p]
            yb_ref[rows_b, cols] = ys[2 * p + 1]
            state_ref[0, p] = states[2 * p]
            state_ref[1, p] = states[2 * p + 1]


def _scan(r, k, v, kk, b, lwf, lwb, seq):
    m = r.shape[0]
    n_chunks = 2
    tb = n_chunks * CHUNK
    nb = seq // tb
    fwd = lambda s, i: (s * nb + i, 0)
    bwd = lambda s, i: (s * nb + nb - 1 - i, 0)
    blk = lambda im: pl.BlockSpec((tb, RWKV_WIDTH), im)
    out = jax.ShapeDtypeStruct((m, RWKV_WIDTH), F32)
    return pl.pallas_call(
        functools.partial(_scan_kernel, n_chunks),
        grid=(m // seq, nb),
        in_specs=[blk(fwd)] * 6 + [blk(bwd)] * 6,
        out_specs=[blk(fwd), blk(bwd)],
        out_shape=[out, out],
        scratch_shapes=[pltpu.VMEM((2, RWKV_WIDTH // LANES, LANES, LANES), F32),
                        pltpu.VMEM((tb, RWKV_WIDTH), F32),
                        pltpu.VMEM((tb, RWKV_WIDTH), F32)],
        compiler_params=_cparams(2),
        name="scan",
    )(r, k, v, kk, b, lwf, r, k, v, kk, b, lwb)


def _post_kernel(seq,
                 yf_ref, yb_ref, g_ref, bonus_ref, cv_ref, cvp_ref, cvn_ref,
                 lng_ref, lnb_ref, cw_ref, x_ref, gt_ref, w_ref, o_ref, mix_ref):
    @pl.when(pl.program_id(1) == 0)
    def _():
        tm = yf_ref.shape[0]
        t0 = (pl.program_id(0) * tm) % seq
        first = t0 == 0
        last = t0 + tm == seq
        y = yf_ref[...] + yb_ref[...]
        mean = _head_sum_bcast(y, two_pass=True) * (1.0 / HEAD)
        d = y - mean
        var = _head_sum_bcast(d * d, two_pass=True) * (1.0 / HEAD)
        yn = d * lax.rsqrt(var + LNX_EPS) * lng_ref[...] + lnb_ref[...]
        mix_ref[:, 0:RWKV_WIDTH] = ((yn + bonus_ref[...]) * g_ref[...]).astype(BF16)

        c1, c2 = slice(CONV_WIDTH, 2 * CONV_WIDTH), slice(2 * CONV_WIDTH, 3 * CONV_WIDTH)
        z = cv_ref[:, c1] * cv_ref[:, c2]
        zp = jnp.where(first, 0.0, cvp_ref[7:8, c1] * cvp_ref[7:8, c2])
        zn = jnp.where(last, 0.0, cvn_ref[0:1, c1] * cvn_ref[0:1, c2])
        ridx = lax.broadcasted_iota(jnp.int32, z.shape, 0)
        z_prev = jnp.where(ridx == 0, zp, pltpu.roll(z, 1, axis=0))
        z_next = jnp.where(ridx == tm - 1, zn, pltpu.roll(z, tm - 1, axis=0))
        cw = cw_ref[...]
        conv = z_prev * cw[0:1] + z * cw[1:2] + z_next * cw[2:3]
        mix_ref[:, RWKV_WIDTH:] = (cv_ref[:, 0:CONV_WIDTH] * conv).astype(BF16)

    o_ref[...] = x_ref[...] + gt_ref[0] * _dot(mix_ref[...], w_ref[...])


def _post(yf, yb, g, bonus, h, lnx_g, lnx_b, conv_w, x2, gt1, w_out, seq):
    m, d = x2.shape
    tm, tn = 512, 512
    per_seq = seq // tm
    nblk8 = m // 8
    row = lambda i, j: (i, 0)
    const = lambda i, j: (0, 0)
    return pl.pallas_call(
        functools.partial(_post_kernel, seq),
        grid=(m // tm, d // tn),
        in_specs=[pl.BlockSpec((tm, RWKV_WIDTH), row)] * 4 + [
            pl.BlockSpec((tm, CONV_COLS), lambda i, j: (i, 1)),
            pl.BlockSpec((8, CONV_COLS), lambda i, j: (jnp.maximum(i * (tm // 8) - 1, 0), 1)),
            pl.BlockSpec((8, CONV_COLS), lambda i, j: (jnp.minimum((i + 1) * (tm // 8), nblk8 - 1), 1)),
            pl.BlockSpec((1, RWKV_WIDTH), const),
            pl.BlockSpec((1, RWKV_WIDTH), const),
            pl.BlockSpec((3, CONV_WIDTH), const),
            pl.BlockSpec((tm, tn), lambda i, j: (i, j)),
            pl.BlockSpec((1, 1, tn), lambda i, j: (i // per_seq, 0, j)),
            pl.BlockSpec((d, tn), lambda i, j: (0, j))],
        out_specs=pl.BlockSpec((tm, tn), lambda i, j: (i, j)),
        out_shape=jax.ShapeDtypeStruct((m, d), F32),
        scratch_shapes=[pltpu.VMEM((tm, d), BF16)],
        compiler_params=_cparams(2),
        name="post",
    )(yf, yb, g, bonus, h, h, h, lnx_g, lnx_b, conv_w, x2, gt1, w_out)


def _ffn_kernel(x_ref, sc_ref, sh_ref, gt_ref, g2_ref, gf_ref, wg_ref, wu_ref, wd_ref, o_ref, hn_ref):
    j = pl.program_id(1)

    @pl.when(j == 0)
    def _():
        x = x_ref[...]
        ms = jnp.mean(x * x, axis=-1, keepdims=True)
        y = x * lax.rsqrt(ms + RMS_EPS) * g2_ref[...]
        hn_ref[...] = (y * (1.0 + sc_ref[0]) + sh_ref[0]).astype(BF16)

    hn = hn_ref[...]
    gate = _dot(hn, wg_ref[...])
    up = _dot(hn, wu_ref[...])
    act = (gate * jax.nn.sigmoid(gate) * up).astype(BF16)

    @pl.when(j == 0)
    def _():
        o_ref[...] = jnp.zeros_like(o_ref)

    n_out = o_ref.shape[1]
    for c0 in range(0, n_out, FFN_DOWN_COLS):
        cols = slice(c0, c0 + FFN_DOWN_COLS)
        o_ref[:, cols] += _dot(act, wd_ref[:, cols])

    @pl.when(j == pl.num_programs(1) - 1)
    def _():
        x = x_ref[...] + gt_ref[0] * o_ref[...]
        ms = jnp.mean(x * x, axis=-1, keepdims=True)
        o_ref[...] = x * lax.rsqrt(ms + RMS_EPS) * gf_ref[...]


def _ffn(x1, sc, sh, gt, g2, gf, wg, wu, wd, seq):
    m, d = x1.shape
    tm, tf = 1024, 512
    per_seq = seq // tm
    row = lambda i, j: (i, 0)
    mod = lambda i, j: (i // per_seq, 0, 0)
    const = lambda i, j: (0, 0)
    return pl.pallas_call(
        _ffn_kernel,
        grid=(m // tm, D_FF // tf),
        in_specs=[pl.BlockSpec((tm, d), row, pipeline_mode=pl.Buffered(1)),
                  pl.BlockSpec((1, 1, d), mod),
                  pl.BlockSpec((1, 1, d), mod),
                  pl.BlockSpec((1, 1, d), mod),
                  pl.BlockSpec((1, d), const),
                  pl.BlockSpec((1, d), const),
                  pl.BlockSpec((d, tf), lambda i, j: (0, j)),
                  pl.BlockSpec((d, tf), lambda i, j: (0, j)),
                  pl.BlockSpec((tf, d), lambda i, j: (j, 0))],
        out_specs=pl.BlockSpec((tm, d), row),
        out_shape=jax.ShapeDtypeStruct((m, d), F32),
        scratch_shapes=[pltpu.VMEM((tm, d), BF16)],
        compiler_params=_cparams(2),
        name="ffn",
    )(x1, sc, sh, gt, g2, gf, wg, wu, wd)


def _layer_weights(w_in, mu_shift, w2_decay, a2, g2_gate):
    d = w_in.shape[0]
    pad = H_COLS - (RKV_COLS + CONV_COLS + SMALL_COLS)
    w_pad = jnp.concatenate([w_in[:, :RKV_COLS], w_in[:, RKV_COLS + SMALL_COLS:],
                             w_in[:, RKV_COLS:RKV_COLS + SMALL_COLS], jnp.zeros((d, pad), F32)],
                            axis=1).astype(BF16)
    mu_rkv = mu_shift[:RKV_COLS].reshape(1, RKV_COLS)
    mu_sm = jnp.concatenate([mu_shift[RKV_COLS:], jnp.zeros((pad,), F32)]).reshape(1, SMALL_PAD)
    zeros = lambda rows, cols: jnp.zeros((rows, cols), F32)
    wfb = jnp.concatenate([
        jnp.concatenate([w2_decay[0], zeros(DECAY_RANK, RWKV_WIDTH)], axis=1),
        jnp.concatenate([zeros(DECAY_RANK, RWKV_WIDTH), w2_decay[1]], axis=1)], axis=0).astype(BF16)
    wa = jnp.concatenate([a2, zeros(LANES - AAA_RANK, RWKV_WIDTH)], axis=0).astype(BF16)
    g_rows = SMALL_PAD - LANES
    wg = jnp.concatenate([zeros(AAA_RANK, RWKV_WIDTH), g2_gate,
                          zeros(g_rows - AAA_RANK - GATE_RANK, RWKV_WIDTH)], axis=0).astype(BF16)
    return w_pad, mu_rkv, mu_sm, wfb, wa, wg


def _group(x, mod, lw):
    nb, seq, d = x.shape
    x2 = x.reshape(nb * seq, d)
    sh1, sc1, gt1, sh2, sc2, gt2 = [mod[:, q * d:(q + 1) * d].reshape(nb, 1, d) for q in range(6)]
    h = _in_proj(x2, sc1, sh1, lw["norm1_g"], lw["w_pad"], seq)
    r, k, v, kk, b, lwf, lwb, g, bonus = _prep(
        h, lw["mu_rkv"], lw["mu_sm"], lw["wfb"], lw["wa"], lw["wg"], lw["w0"], lw["a0"],
        lw["k_k"], lw["k_a"], lw["r_k"], seq)
    yf, yb = _scan(r, k, v, kk, b, lwf, lwb, seq)
    x1 = _post(yf, yb, g, bonus, h, lw["lnx_g"], lw["lnx_b"], lw["conv_w"], x2, gt1, lw["w_out"], seq)
    out = _ffn(x1, sc2, sh2, gt2, lw["norm2_g"], lw["norm_f_g"], lw["wg_ffn"], lw["wu_ffn"], lw["wd_ffn"], seq)
    return out.reshape(nb, seq, d)


def kernel(x_prompt, x_sample, c_prompt, c_sample, w_ada, b_ada, norm1_g, w_in, mu_shift, w0_decay, w2_decay, a0, a2, g2_gate, k_k, k_a, r_k, lnx_g, lnx_b, conv_w, w_out, norm2_g, w_ffn_gate, w_ffn_up, w_ffn_down, norm_f_g):
    assert w_ada.shape[0] == 1, "single-layer block"
    l = 0
    row = lambda t: t.reshape(1, -1)
    w_pad, mu_rkv, mu_sm, wfb, wa, wg = _layer_weights(w_in[l], mu_shift[l], w2_decay[l], a2[l], g2_gate[l])
    lw = dict(w_pad=w_pad, mu_rkv=mu_rkv, mu_sm=mu_sm, wfb=wfb, wa=wa, wg=wg,
              norm1_g=row(norm1_g[l]), w0=w0_decay[l], a0=row(a0[l]), k_k=row(k_k[l]), k_a=row(k_a[l]),
              r_k=row(r_k[l]), lnx_g=row(lnx_g[l]), lnx_b=row(lnx_b[l]), conv_w=conv_w[l],
              w_out=w_out[l].astype(BF16), norm2_g=row(norm2_g[l]), norm_f_g=row(norm_f_g),
              wg_ffn=w_ffn_gate[l].astype(BF16), wu_ffn=w_ffn_up[l].astype(BF16),
              wd_ffn=w_ffn_down[l].astype(BF16))
    n_prompt = c_prompt.shape[0]
    mod = _mod(jnp.concatenate([c_prompt, c_sample], axis=0), w_ada[l], b_ada[l])
    y_prompt = _group(x_prompt, mod[:n_prompt], lw)
    y_sample = _group(x_sample, mod[n_prompt:], lw)
    return (y_prompt, y_sample)
```

```python
import functools
import math

import jax
import jax.numpy as jnp
from jax import lax
from jax.experimental import pallas as pl
from jax.experimental.pallas import tpu as pltpu

F32 = jnp.float32
BF16 = jnp.bfloat16

D_MODEL = 2048
RWKV_WIDTH = 1024
CONV_WIDTH = 1024
HEAD = 64
DECAY_RANK = 64
AAA_RANK = 64
GATE_RANK = 160
SMALL_COLS = 2 * DECAY_RANK + AAA_RANK + GATE_RANK
SMALL_PAD = 512
RKV_COLS = 3 * RWKV_WIDTH
CONV_COLS = 3 * CONV_WIDTH
H_COLS = RKV_COLS + CONV_COLS + SMALL_PAD
D_FF = 5632
RMS_EPS = 1e-6
LNX_EPS = 64e-5

CHUNK = 64
LANES = 128
VMEM_LIMIT = 56 * 1024 * 1024
FFN_DOWN_COLS = 512
POST_OUT_COLS = 512


def _cparams(n_axes):
    return pltpu.CompilerParams(dimension_semantics=("arbitrary",) * n_axes,
                                vmem_limit_bytes=VMEM_LIMIT)


def _dot(a, b):
    return jnp.dot(a, b, preferred_element_type=F32)


def _dot_nt(a, b):
    return lax.dot_general(a, b, (((1,), (1,)), ((), ())), preferred_element_type=F32)


def _split_bf16(x):
    hi = x.astype(BF16)
    lo = (x - hi.astype(F32)).astype(BF16)
    return hi, lo


def _head_sum_bcast(x, two_pass):
    n = 2 * LANES
    row = lax.broadcasted_iota(jnp.int32, (n, n), 0) >> 6
    col = lax.broadcasted_iota(jnp.int32, (n, n), 1) >> 6
    ones_bd = jnp.where(row == col, 1.0, 0.0).astype(BF16)
    outs = []
    for q in range(x.shape[1] // n):
        xs = x[:, q * n:(q + 1) * n]
        if two_pass:
            hi, lo = _split_bf16(xs)
            outs.append(_dot(hi, ones_bd) + _dot(lo, ones_bd))
        else:
            outs.append(_dot(xs.astype(BF16), ones_bd))
    return jnp.concatenate(outs, axis=1)


def _mod_kernel(c_ref, w_ref, b_ref, o_ref):
    c = c_ref[...]
    s = c * jax.nn.sigmoid(c)
    o_ref[...] = _dot(s.astype(BF16), w_ref[...].astype(BF16)) + b_ref[...]


def _mod(c, w_ada, b_ada):
    nb, d = c.shape
    n = w_ada.shape[1]
    tn = 1024
    return pl.pallas_call(
        _mod_kernel,
        grid=(n // tn,),
        in_specs=[pl.BlockSpec((nb, d), lambda j: (0, 0)),
                  pl.BlockSpec((d, tn), lambda j: (0, j)),
                  pl.BlockSpec((1, tn), lambda j: (0, j))],
        out_specs=pl.BlockSpec((nb, tn), lambda j: (0, j)),
        out_shape=jax.ShapeDtypeStruct((nb, n), F32),
        compiler_params=_cparams(1),
        name="mod",
    )(c, w_ada, b_ada.reshape(1, n))


def _inproj_kernel(x_ref, sc_ref, sh_ref, g_ref, w_ref, o_ref, hn_ref):
    @pl.when(pl.program_id(1) == 0)
    def _():
        x = x_ref[...]
        ms = jnp.mean(x * x, axis=-1, keepdims=True)
        y = x * lax.rsqrt(ms + RMS_EPS) * g_ref[...]
        hn_ref[...] = (y * (1.0 + sc_ref[0]) + sh_ref[0]).astype(BF16)

    o_ref[...] = _dot(hn_ref[...], w_ref[...])


def _in_proj(x2, sc, sh, g1, w_pad, seq):
    m, d = x2.shape
    tm, tn = 1024, 512
    per_seq = seq // tm
    return pl.pallas_call(
        _inproj_kernel,
        grid=(m // tm, H_COLS // tn),
        in_specs=[pl.BlockSpec((tm, d), lambda i, j: (i, 0)),
                  pl.BlockSpec((1, 1, d), lambda i, j: (i // per_seq, 0, 0)),
                  pl.BlockSpec((1, 1, d), lambda i, j: (i // per_seq, 0, 0)),
                  pl.BlockSpec((1, d), lambda i, j: (0, 0)),
                  pl.BlockSpec((d, tn), lambda i, j: (0, j))],
        out_specs=pl.BlockSpec((tm, tn), lambda i, j: (i, j)),
        out_shape=jax.ShapeDtypeStruct((m, H_COLS), F32),
        scratch_shapes=[pltpu.VMEM((tm, d), BF16)],
        compiler_params=_cparams(2),
        name="in_proj",
    )(x2, sc, sh, g1, w_pad)


def _shifted(main, prev8, next8, first, last):
    rows, cols = main.shape
    ridx = lax.broadcasted_iota(jnp.int32, (rows, cols), 0)
    cidx = lax.broadcasted_iota(jnp.int32, (rows, cols), 1)
    prev_row = jnp.where(first, 0.0, prev8[7:8, :])
    next_row = jnp.where(last, 0.0, next8[0:1, :])
    prv = jnp.where(ridx == 0, prev_row, pltpu.roll(main, 1, axis=0))
    nxt = jnp.where(ridx == rows - 1, next_row, pltpu.roll(main, rows - 1, axis=0))
    return jnp.where((cidx & 1) == 0, prv, nxt)


def _prep_kernel(seq,
                 rkv_ref, rkvp_ref, rkvn_ref, sm_ref, smp_ref, smn_ref,
                 mu_rkv_ref, mu_sm_ref, wfb_ref, wa_ref, wg_ref,
                 w0_ref, a0_ref, kk_ref, ka_ref, rk_ref,
                 r_out, k_out, v_out, kk_out, b_out, lwf_out, lwb_out, g_out, bonus_out):
    tp = rkv_ref.shape[0]
    t0 = (pl.program_id(0) * tp) % seq
    first = t0 == 0
    last = t0 + tp == seq

    sm = sm_ref[...]
    xs_sm = sm + mu_sm_ref[...] * (_shifted(sm, smp_ref[...], smn_ref[...], first, last) - sm)
    lora_fb = _dot(jnp.tanh(xs_sm[:, 0:LANES]).astype(BF16), wfb_ref[...])
    lora_a = _dot(xs_sm[:, LANES:2 * LANES].astype(BF16), wa_ref[...])
    g = _dot(jax.nn.sigmoid(xs_sm[:, LANES:]).astype(BF16), wg_ref[...])

    decay_scale = math.exp(-0.5)
    w0 = w0_ref[...]
    lwf_out[...] = -decay_scale * jax.nn.sigmoid(w0[0:1] + lora_fb[:, :RWKV_WIDTH])
    lwb_out[...] = -decay_scale * jax.nn.sigmoid(w0[1:2] + lora_fb[:, RWKV_WIDTH:])
    g_out[...] = g.astype(g_out.dtype)
    a = jax.nn.sigmoid(a0_ref[...] + lora_a)

    def shifted_cols(c0):
        sl = slice(c0, c0 + RWKV_WIDTH)
        main = rkv_ref[:, sl]
        mu = mu_rkv_ref[:, sl]
        return main + mu * (_shifted(main, rkvp_ref[:, sl], rkvn_ref[:, sl], first, last) - main)

    xk = shifted_cols(RWKV_WIDTH)
    kk_raw = xk * kk_ref[...]
    ss = _head_sum_bcast(kk_raw * kk_raw, two_pass=True)
    kk = kk_raw / jnp.maximum(jnp.sqrt(ss), 1e-12)
    k = xk * (1.0 + (a - 1.0) * ka_ref[...])
    kk_out[...] = kk.astype(kk_out.dtype)
    b_out[...] = (kk * a).astype(b_out.dtype)
    k_out[...] = k.astype(k_out.dtype)
    r = shifted_cols(0)
    r_out[...] = r.astype(r_out.dtype)
    rk_sum = _head_sum_bcast(r * k * rk_ref[...], two_pass=True)
    v = shifted_cols(2 * RWKV_WIDTH)
    v_out[...] = v.astype(v_out.dtype)
    bonus_out[...] = (rk_sum * v).astype(bonus_out.dtype)


def _prep(h, mu_rkv, mu_sm, wfb, wa, wg, w0, a0, k_k, k_a, r_k, seq):
    m = h.shape[0]
    tp = 256
    nblk8 = m // 8
    small_blk = (RKV_COLS + CONV_COLS) // SMALL_PAD
    row = lambda i: (i, 0)
    prev8 = lambda i: (jnp.maximum(i * (tp // 8) - 1, 0), 0)
    next8 = lambda i: (jnp.minimum((i + 1) * (tp // 8), nblk8 - 1), 0)
    const = lambda i: (0, 0)
    out_dtypes = [BF16] * 5 + [F32, F32, BF16, BF16]
    return pl.pallas_call(
        functools.partial(_prep_kernel, seq),
        grid=(m // tp,),
        in_specs=[pl.BlockSpec((tp, RKV_COLS), row),
                  pl.BlockSpec((8, RKV_COLS), prev8),
                  pl.BlockSpec((8, RKV_COLS), next8),
                  pl.BlockSpec((tp, SMALL_PAD), lambda i: (i, small_blk)),
                  pl.BlockSpec((8, SMALL_PAD), lambda i: (prev8(i)[0], small_blk)),
                  pl.BlockSpec((8, SMALL_PAD), lambda i: (next8(i)[0], small_blk)),
                  pl.BlockSpec((1, RKV_COLS), const),
                  pl.BlockSpec((1, SMALL_PAD), const),
                  pl.BlockSpec(wfb.shape, const),
                  pl.BlockSpec(wa.shape, const),
                  pl.BlockSpec(wg.shape, const),
                  pl.BlockSpec((2, RWKV_WIDTH), const),
                  pl.BlockSpec((1, RWKV_WIDTH), const),
                  pl.BlockSpec((1, RWKV_WIDTH), const),
                  pl.BlockSpec((1, RWKV_WIDTH), const),
                  pl.BlockSpec((1, RWKV_WIDTH), const)],
        out_specs=[pl.BlockSpec((tp, RWKV_WIDTH), row)] * 9,
        out_shape=[jax.ShapeDtypeStruct((m, RWKV_WIDTH), dt) for dt in out_dtypes],
        compiler_params=_cparams(1),
        name="prep",
    )(h, h, h, h, h, h, mu_rkv, mu_sm, wfb, wa, wg, w0, a0, k_k, k_a, r_k)


def _block_diag(x, lane_lo):
    return jnp.concatenate([jnp.where(lane_lo, x, 0.0).astype(BF16),
                            jnp.where(lane_lo, 0.0, x).astype(BF16)], axis=0)


def _chunk_step(chains, common):
    eye2, lane_lo, bd_mask = common
    bd = lambda x: _block_diag(x, lane_lo)
    n = len(chains)
    at, rt, kt, bt, g_all = [], [], [], [], []
    for (r, k, v, kk, b, lw, cum, _, _, bwd) in chains:
        gq = jnp.exp(cum)
        gk = jnp.exp(-cum)
        g_all.append(gq[0:1] if bwd else gq[CHUNK - 1:CHUNK])
        at.append(-(kk * jnp.exp(cum - lw)))
        rt.append(r * gq)
        kt.append(k * gk)
        bt.append(b * gk)

    a_ak, a_ab, a_rk, a_rb = [], [], [], []
    for c in range(n):
        tri_strict, tri_incl = chains[c][8]
        lhs = jnp.concatenate([at[c], rt[c]], axis=0).astype(BF16)
        rhs = jnp.concatenate([bd(kt[c]), bd(bt[c])], axis=0)
        big = _dot_nt(lhs, rhs)
        a_ak.append(jnp.where(tri_strict, big[0:CHUNK, 0:LANES], 0.0))
        a_ab.append(jnp.where(tri_strict, big[0:CHUNK, LANES:], 0.0))
        a_rk.append(jnp.where(tri_incl, big[CHUNK:, 0:LANES], 0.0))
        a_rb.append(jnp.where(tri_incl, big[CHUNK:, LANES:], 0.0))

    q = [eye2 + a for a in a_ab]
    p = [_dot(a.astype(BF16), bd(a)) for a in a_ab]
    xy = [_dot(jnp.concatenate([a_ak[c], a_rk[c]], axis=0).astype(BF16), bd(chains[c][2])) for c in range(n)]
    for _ in range(4):
        pq_pp = [_dot(p[c].astype(BF16), jnp.concatenate([bd(q[c]), bd(p[c])], axis=1)) for c in range(n)]
        q = [q[c] + pq_pp[c][:, 0:LANES] for c in range(n)]
        p = [pq_pp[c][:, LANES:] for c in range(n)]
    q = [q[c] + _dot(p[c].astype(BF16), bd(q[c])) for c in range(n)]

    wu = [_dot(q[c].astype(BF16), jnp.concatenate([bd(at[c]), bd(xy[c][0:CHUNK])], axis=1)) for c in range(n)]
    s1 = [_dot_nt(jnp.concatenate([wu[c][:, 0:LANES], rt[c]], axis=0).astype(BF16), chains[c][7].astype(BF16))
          for c in range(n)]
    u = [wu[c][:, LANES:] + s1[c][0:CHUNK] for c in range(n)]
    ys = [s1[c][CHUNK:] + xy[c][CHUNK:] + _dot(a_rb[c].astype(BF16), bd(u[c])) for c in range(n)]
    states = []
    for c in range(n):
        vu_t = jnp.concatenate([chains[c][2], u[c]], axis=0).T.astype(BF16)
        kb = jnp.concatenate([kt[c], bt[c]], axis=0).astype(BF16)
        states.append(jnp.where(bd_mask, (chains[c][7] + _dot(vu_t, kb)) * g_all[c], 0.0))
    return ys, states


def _scan_kernel(n_chunks,
                 rf_ref, kf_ref, vf_ref, kkf_ref, bf_ref, lwf_ref,
                 rb_ref, kb_ref, vb_ref, kkb_ref, bb_ref, lwb_ref,
                 yf_ref, yb_ref, state_ref, cumf_ref, cumb_ref):
    @pl.when(pl.program_id(1) == 0)
    def _():
        state_ref[...] = jnp.zeros_like(state_ref)

    n_pairs = rf_ref.shape[1] // LANES
    tb = n_chunks * CHUNK
    ri = lax.broadcasted_iota(jnp.int32, (CHUNK, LANES), 0)
    ci = lax.broadcasted_iota(jnp.int32, (CHUNK, LANES), 1) & (CHUNK - 1)
    lane_lo = lax.broadcasted_iota(jnp.int32, (CHUNK, LANES), 1) < CHUNK
    eye2 = jnp.where(ri == ci, 1.0, 0.0)
    br = lax.broadcasted_iota(jnp.int32, (LANES, LANES), 0) >> 6
    bc = lax.broadcasted_iota(jnp.int32, (LANES, LANES), 1) >> 6
    common = (eye2, lane_lo, br == bc)
    tri_f = (ci < ri, ci <= ri)
    tri_b = (ci > ri, ci >= ri)

    tr = lax.broadcasted_iota(jnp.int32, (tb, tb), 0)
    tc = lax.broadcasted_iota(jnp.int32, (tb, tb), 1)
    same = (tr >> 6) == (tc >> 6)
    ones_f = jnp.where(same & (tc <= tr), 1.0, 0.0).astype(BF16)
    ones_b = jnp.where(same & (tc >= tr), 1.0, 0.0).astype(BF16)

    def cumsum(ones, lw):
        hi, lo = _split_bf16(lw)
        return _dot(ones, hi) + _dot(ones, lo)

    cumf_ref[...] = cumsum(ones_f, lwf_ref[...])
    cumb_ref[...] = cumsum(ones_b, lwb_ref[...])

    for step in range(n_chunks):
        rows_f = slice(step * CHUNK, (step + 1) * CHUNK)
        rows_b = slice((n_chunks - 1 - step) * CHUNK, (n_chunks - step) * CHUNK)
        chains = []
        for p in range(n_pairs):
            cols = slice(p * LANES, (p + 1) * LANES)
            tok_f = [ref[rows_f, cols].astype(F32) for ref in (rf_ref, kf_ref, vf_ref, kkf_ref, bf_ref)]
            tok_b = [ref[rows_b, cols].astype(F32) for ref in (rb_ref, kb_ref, vb_ref, kkb_ref, bb_ref)]
            chains.append((*tok_f, lwf_ref[rows_f, cols], cumf_ref[rows_f, cols], state_ref[0, p], tri_f, False))
            chains.append((*tok_b, lwb_ref[rows_b, cols], cumb_ref[rows_b, cols], state_ref[1, p], tri_b, True))
        ys, states = _chunk_step(chains, common)
        for p in range(n_pairs):
            cols = slice(p * LANES, (p + 1) * LANES)
            yf_ref[rows_f, cols] = ys[2 * p].astype(yf_ref.dtype)
            yb_ref[rows_b, cols] = ys[2 * p + 1].astype(yb_ref.dtype)
            state_ref[0, p] = states[2 * p]
            state_ref[1, p] = states[2 * p + 1]


def _scan(r, k, v, kk, b, lwf, lwb, seq):
    m = r.shape[0]
    n_chunks = 2
    tb = n_chunks * CHUNK
    nb = seq // tb
    fwd = lambda s, i: (s * nb + i, 0)
    bwd = lambda s, i: (s * nb + nb - 1 - i, 0)
    blk = lambda im: pl.BlockSpec((tb, RWKV_WIDTH), im)
    out = jax.ShapeDtypeStruct((m, RWKV_WIDTH), BF16)
    return pl.pallas_call(
        functools.partial(_scan_kernel, n_chunks),
        grid=(m // seq, nb),
        in_specs=[blk(fwd)] * 6 + [blk(bwd)] * 6,
        out_specs=[blk(fwd), blk(bwd)],
        out_shape=[out, out],
        scratch_shapes=[pltpu.VMEM((2, RWKV_WIDTH // LANES, LANES, LANES), F32),
                        pltpu.VMEM((tb, RWKV_WIDTH), F32),
                        pltpu.VMEM((tb, RWKV_WIDTH), F32)],
        compiler_params=_cparams(2),
        name="scan",
    )(r, k, v, kk, b, lwf, r, k, v, kk, b, lwb)


def _post_kernel(seq,
                 yf_ref, yb_ref, g_ref, bonus_ref, cv_ref, cvp_ref, cvn_ref,
                 lng_ref, lnb_ref, cw_ref, x_ref, gt_ref, w_ref, o_ref, mix_ref):
    tm = yf_ref.shape[0]
    t0 = (pl.program_id(0) * tm) % seq
    first = t0 == 0
    last = t0 + tm == seq
    y = yf_ref[...].astype(F32) + yb_ref[...].astype(F32)
    mean = _head_sum_bcast(y, two_pass=True) * (1.0 / HEAD)
    d = y - mean
    var = _head_sum_bcast(d * d, two_pass=True) * (1.0 / HEAD)
    yn = d * lax.rsqrt(var + LNX_EPS) * lng_ref[...] + lnb_ref[...]
    mix_ref[:, 0:RWKV_WIDTH] = ((yn + bonus_ref[...].astype(F32)) * g_ref[...].astype(F32)).astype(BF16)

    c1, c2 = slice(CONV_WIDTH, 2 * CONV_WIDTH), slice(2 * CONV_WIDTH, 3 * CONV_WIDTH)
    z = cv_ref[:, c1] * cv_ref[:, c2]
    zp = jnp.where(first, 0.0, cvp_ref[7:8, c1] * cvp_ref[7:8, c2])
    zn = jnp.where(last, 0.0, cvn_ref[0:1, c1] * cvn_ref[0:1, c2])
    ridx = lax.broadcasted_iota(jnp.int32, z.shape, 0)
    z_prev = jnp.where(ridx == 0, zp, pltpu.roll(z, 1, axis=0))
    z_next = jnp.where(ridx == tm - 1, zn, pltpu.roll(z, tm - 1, axis=0))
    cw = cw_ref[...]
    conv = z_prev * cw[0:1] + z * cw[1:2] + z_next * cw[2:3]
    mix_ref[:, RWKV_WIDTH:] = (cv_ref[:, 0:CONV_WIDTH] * conv).astype(BF16)

    gt = gt_ref[0]
    for c0 in range(0, o_ref.shape[1], POST_OUT_COLS):
        cols = slice(c0, c0 + POST_OUT_COLS)
        o_ref[:, cols] = x_ref[:, cols] + gt[:, cols] * _dot(mix_ref[...], w_ref[:, cols])


def _post(yf, yb, g, bonus, h, lnx_g, lnx_b, conv_w, x2, gt1, w_out, seq):
    m, d = x2.shape
    tm = 256
    per_seq = seq // tm
    nblk8 = m // 8
    row = lambda i: (i, 0)
    const = lambda i: (0, 0)
    return pl.pallas_call(
        functools.partial(_post_kernel, seq),
        grid=(m // tm,),
        in_specs=[pl.BlockSpec((tm, RWKV_WIDTH), row)] * 4 + [
            pl.BlockSpec((tm, CONV_COLS), lambda i: (i, 1)),
            pl.BlockSpec((8, CONV_COLS), lambda i: (jnp.maximum(i * (tm // 8) - 1, 0), 1)),
            pl.BlockSpec((8, CONV_COLS), lambda i: (jnp.minimum((i + 1) * (tm // 8), nblk8 - 1), 1)),
            pl.BlockSpec((1, RWKV_WIDTH), const),
            pl.BlockSpec((1, RWKV_WIDTH), const),
            pl.BlockSpec((3, CONV_WIDTH), const),
            pl.BlockSpec((tm, d), row),
            pl.BlockSpec((1, 1, d), lambda i: (i // per_seq, 0, 0)),
            pl.BlockSpec((d, d), const, pipeline_mode=pl.Buffered(1))],
        out_specs=pl.BlockSpec((tm, d), row),
        out_shape=jax.ShapeDtypeStruct((m, d), F32),
        scratch_shapes=[pltpu.VMEM((tm, d), BF16)],
        compiler_params=_cparams(1),
        name="post",
    )(yf, yb, g, bonus, h, h, h, lnx_g, lnx_b, conv_w, x2, gt1, w_out)


def _ffn_kernel(x_ref, sc_ref, sh_ref, gt_ref, g2_ref, gf_ref, wg_ref, wu_ref, wd_ref, o_ref, hn_ref):
    j = pl.program_id(1)

    @pl.when(j == 0)
    def _():
        x = x_ref[...]
        ms = jnp.mean(x * x, axis=-1, keepdims=True)
        y = x * lax.rsqrt(ms + RMS_EPS) * g2_ref[...]
        hn_ref[...] = (y * (1.0 + sc_ref[0]) + sh_ref[0]).astype(BF16)

    hn = hn_ref[...]
    gate = _dot(hn, wg_ref[...])
    up = _dot(hn, wu_ref[...])
    act = (gate * jax.nn.sigmoid(gate) * up).astype(BF16)

    @pl.when(j == 0)
    def _():
        o_ref[...] = jnp.zeros_like(o_ref)

    n_out = o_ref.shape[1]
    for c0 in range(0, n_out, FFN_DOWN_COLS):
        cols = slice(c0, c0 + FFN_DOWN_COLS)
        o_ref[:, cols] += _dot(act, wd_ref[:, cols])

    @pl.when(j == pl.num_programs(1) - 1)
    def _():
        x = x_ref[...] + gt_ref[0] * o_ref[...]
        ms = jnp.mean(x * x, axis=-1, keepdims=True)
        o_ref[...] = x * lax.rsqrt(ms + RMS_EPS) * gf_ref[...]


def _ffn(x1, sc, sh, gt, g2, gf, wg, wu, wd, seq):
    m, d = x1.shape
    tm, tf = 1024, 512
    per_seq = seq // tm
    row = lambda i, j: (i, 0)
    mod = lambda i, j: (i // per_seq, 0, 0)
    const = lambda i, j: (0, 0)
    return pl.pallas_call(
        _ffn_kernel,
        grid=(m // tm, D_FF // tf),
        in_specs=[pl.BlockSpec((tm, d), row, pipeline_mode=pl.Buffered(1)),
                  pl.BlockSpec((1, 1, d), mod),
                  pl.BlockSpec((1, 1, d), mod),
                  pl.BlockSpec((1, 1, d), mod),
                  pl.BlockSpec((1, d), const),
                  pl.BlockSpec((1, d), const),
                  pl.BlockSpec((d, tf), lambda i, j: (0, j)),
                  pl.BlockSpec((d, tf), lambda i, j: (0, j)),
                  pl.BlockSpec((tf, d), lambda i, j: (j, 0))],
        out_specs=pl.BlockSpec((tm, d), row),
        out_shape=jax.ShapeDtypeStruct((m, d), F32),
        scratch_shapes=[pltpu.VMEM((tm, d), BF16)],
        compiler_params=_cparams(2),
        name="ffn",
    )(x1, sc, sh, gt, g2, gf, wg, wu, wd)


def _layer_weights(w_in, mu_shift, w2_decay, a2, g2_gate):
    d = w_in.shape[0]
    pad = H_COLS - (RKV_COLS + CONV_COLS + SMALL_COLS)
    w_pad = jnp.concatenate([w_in[:, :RKV_COLS], w_in[:, RKV_COLS + SMALL_COLS:],
                             w_in[:, RKV_COLS:RKV_COLS + SMALL_COLS], jnp.zeros((d, pad), F32)],
                            axis=1).astype(BF16)
    mu_rkv = mu_shift[:RKV_COLS].reshape(1, RKV_COLS)
    mu_sm = jnp.concatenate([mu_shift[RKV_COLS:], jnp.zeros((pad,), F32)]).reshape(1, SMALL_PAD)
    zeros = lambda rows, cols: jnp.zeros((rows, cols), F32)
    wfb = jnp.concatenate([
        jnp.concatenate([w2_decay[0], zeros(DECAY_RANK, RWKV_WIDTH)], axis=1),
        jnp.concatenate([zeros(DECAY_RANK, RWKV_WIDTH), w2_decay[1]], axis=1)], axis=0).astype(BF16)
    wa = jnp.concatenate([a2, zeros(LANES - AAA_RANK, RWKV_WIDTH)], axis=0).astype(BF16)
    g_rows = SMALL_PAD - LANES
    wg = jnp.concatenate([zeros(AAA_RANK, RWKV_WIDTH), g2_gate,
                          zeros(g_rows - AAA_RANK - GATE_RANK, RWKV_WIDTH)], axis=0).astype(BF16)
    return w_pad, mu_rkv, mu_sm, wfb, wa, wg


def _group(x, mod, lw):
    nb, seq, d = x.shape
    x2 = x.reshape(nb * seq, d)
    sh1, sc1, gt1, sh2, sc2, gt2 = [mod[:, q * d:(q + 1) * d].reshape(nb, 1, d) for q in range(6)]
    h = _in_proj(x2, sc1, sh1, lw["norm1_g"], lw["w_pad"], seq)
    r, k, v, kk, b, lwf, lwb, g, bonus = _prep(
        h, lw["mu_rkv"], lw["mu_sm"], lw["wfb"], lw["wa"], lw["wg"], lw["w0"], lw["a0"],
        lw["k_k"], lw["k_a"], lw["r_k"], seq)
    yf, yb = _scan(r, k, v, kk, b, lwf, lwb, seq)
    x1 = _post(yf, yb, g, bonus, h, lw["lnx_g"], lw["lnx_b"], lw["conv_w"], x2, gt1, lw["w_out"], seq)
    out = _ffn(x1, sc2, sh2, gt2, lw["norm2_g"], lw["norm_f_g"], lw["wg_ffn"], lw["wu_ffn"], lw["wd_ffn"], seq)
    return out.reshape(nb, seq, d)


def kernel(x_prompt, x_sample, c_prompt, c_sample, w_ada, b_ada, norm1_g, w_in, mu_shift, w0_decay, w2_decay, a0, a2, g2_gate, k_k, k_a, r_k, lnx_g, lnx_b, conv_w, w_out, norm2_g, w_ffn_gate, w_ffn_up, w_ffn_down, norm_f_g):
    assert w_ada.shape[0] == 1, "single-layer block"
    l = 0
    row = lambda t: t.reshape(1, -1)
    w_pad, mu_rkv, mu_sm, wfb, wa, wg = _layer_weights(w_in[l], mu_shift[l], w2_decay[l], a2[l], g2_gate[l])
    lw = dict(w_pad=w_pad, mu_rkv=mu_rkv, mu_sm=mu_sm, wfb=wfb, wa=wa, wg=wg,
              norm1_g=row(norm1_g[l]), w0=w0_decay[l], a0=row(a0[l]), k_k=row(k_k[l]), k_a=row(k_a[l]),
              r_k=row(r_k[l]), lnx_g=row(lnx_g[l]), lnx_b=row(lnx_b[l]), conv_w=conv_w[l],
              w_out=w_out[l].astype(BF16), norm2_g=row(norm2_g[l]), norm_f_g=row(norm_f_g),
              wg_ffn=w_ffn_gate[l].astype(BF16), wu_ffn=w_ffn_up[l].astype(BF16),
              wd_ffn=w_ffn_down[l].astype(BF16))
    n_prompt = c_prompt.shape[0]
    mod = _mod(jnp.concatenate([c_prompt, c_sample], axis=0), w_ada[l], b_ada[l])
    y_prompt = _group(x_prompt, mod[:n_prompt], lw)
    y_sample = _group(x_sample, mod[n_prompt:], lw)
    return (y_prompt, y_sample)
```

```python
import functools
import math

import jax
import jax.numpy as jnp
from jax import lax
from jax.experimental import pallas as pl
from jax.experimental.pallas import tpu as pltpu

F32 = jnp.float32
BF16 = jnp.bfloat16

D_MODEL = 2048
RWKV_WIDTH = 1024
CONV_WIDTH = 1024
HEAD = 64
DECAY_RANK = 64
AAA_RANK = 64
GATE_RANK = 160
SMALL_COLS = 2 * DECAY_RANK + AAA_RANK + GATE_RANK
SMALL_PAD = 512
RKV_COLS = 3 * RWKV_WIDTH
CONV_COLS = 3 * CONV_WIDTH
H_COLS = RKV_COLS + CONV_COLS + SMALL_PAD
D_FF = 5632
RMS_EPS = 1e-6
LNX_EPS = 64e-5

CHUNK = 64
LANES = 128
MXU_COLS = 256
VMEM_LIMIT = 56 * 1024 * 1024
FFN_DOWN_COLS = 512
POST_OUT_COLS = 512


def _cparams(n_axes):
    return pltpu.CompilerParams(dimension_semantics=("arbitrary",) * n_axes,
                                vmem_limit_bytes=VMEM_LIMIT)


def _dot(a, b):
    return jnp.dot(a, b, preferred_element_type=F32)


def _dot_nt(a, b):
    return lax.dot_general(a, b, (((1,), (1,)), ((), ())), preferred_element_type=F32)


def _split_bf16(x):
    hi = x.astype(BF16)
    lo = (x - hi.astype(F32)).astype(BF16)
    return hi, lo


def _head_sum_bcast(x, two_pass):
    n = 2 * LANES
    row = lax.broadcasted_iota(jnp.int32, (n, n), 0) >> 6
    col = lax.broadcasted_iota(jnp.int32, (n, n), 1) >> 6
    ones_bd = jnp.where(row == col, 1.0, 0.0).astype(BF16)
    outs = []
    for q in range(x.shape[1] // n):
        xs = x[:, q * n:(q + 1) * n]
        if two_pass:
            hi, lo = _split_bf16(xs)
            outs.append(_dot(hi, ones_bd) + _dot(lo, ones_bd))
        else:
            outs.append(_dot(xs.astype(BF16), ones_bd))
    return jnp.concatenate(outs, axis=1)


def _mod_kernel(c_ref, w_ref, b_ref, o_ref):
    c = c_ref[...]
    s = c * jax.nn.sigmoid(c)
    o_ref[...] = _dot(s.astype(BF16), w_ref[...].astype(BF16)) + b_ref[...]


def _mod(c, w_ada, b_ada):
    nb, d = c.shape
    n = w_ada.shape[1]
    tn = 1024
    return pl.pallas_call(
        _mod_kernel,
        grid=(n // tn,),
        in_specs=[pl.BlockSpec((nb, d), lambda j: (0, 0)),
                  pl.BlockSpec((d, tn), lambda j: (0, j)),
                  pl.BlockSpec((1, tn), lambda j: (0, j))],
        out_specs=pl.BlockSpec((nb, tn), lambda j: (0, j)),
        out_shape=jax.ShapeDtypeStruct((nb, n), F32),
        compiler_params=_cparams(1),
        name="mod",
    )(c, w_ada, b_ada.reshape(1, n))


def _token_shift_mix(h, prev_row, next_row, mu):
    rows, cols = h.shape
    r8 = lax.broadcasted_iota(jnp.int32, (8, cols), 0)
    even = (lax.broadcasted_iota(jnp.int32, (rows, cols), 1) & 1) == 0
    prv = pltpu.roll(h, 1, axis=0)
    prv = jnp.concatenate([jnp.where(r8 == 0, prev_row, prv[0:8]), prv[8:]], axis=0)
    nxt = pltpu.roll(h, rows - 1, axis=0)
    nxt = jnp.concatenate([nxt[:rows - 8], jnp.where(r8 == 7, next_row, nxt[rows - 8:])], axis=0)
    return h + mu * (jnp.where(even, prv, nxt) - h)


def _inproj_kernel(seq, n_split, shift_tiles, x_ref, xp_ref, xn_ref, sc_ref, sh_ref, g_ref, w_ref, mu_ref,
                   o_ref, hn_ref):
    tm = x_ref.shape[0]
    tn = o_ref.shape[1]
    t0 = (pl.program_id(0) * tm) % seq
    j = pl.program_id(1)
    is_shift = functools.reduce(lambda a, b: a | b, [j == t for t in shift_tiles if t != 0])

    def normed(x):
        ms = jnp.mean(x * x, axis=-1, keepdims=True)
        y = x * lax.rsqrt(ms + RMS_EPS) * g_ref[...]
        return y * (1.0 + sc_ref[0]) + sh_ref[0]

    def mix_store(h_all, cols):
        prev_row = jnp.where(t0 == 0, 0.0, h_all[tm + 7:tm + 8])
        next_row = jnp.where(t0 + tm == seq, 0.0, h_all[tm + 8:tm + 9])
        o_ref[:, cols] = _token_shift_mix(h_all[0:tm], prev_row, next_row, mu_ref[:, cols])

    @pl.when(j == 0)
    def _():
        hn_ref[tm:, :] = jnp.concatenate([normed(xp_ref[...]), normed(xn_ref[...])], axis=0).astype(BF16)
        rows = tm // n_split
        parts = []
        for q in range(n_split):
            sl = slice(q * rows, (q + 1) * rows)
            hn_ref[sl, :] = normed(x_ref[sl, :]).astype(BF16)
            sl = slice(q * rows, (q + 1) * rows if q + 1 < n_split else tm + 16)
            parts.append(_dot(hn_ref[sl, :], w_ref[...]))
        h_all = jnp.concatenate(parts, axis=0)
        for c0 in range(0, tn, MXU_COLS):
            mix_store(h_all[:, c0:c0 + MXU_COLS], slice(c0, c0 + MXU_COLS))

    @pl.when(is_shift)
    def _():
        for c0 in range(0, tn, MXU_COLS):
            cols = slice(c0, c0 + MXU_COLS)
            mix_store(_dot(hn_ref[...], w_ref[:, cols]), cols)

    @pl.when(jnp.logical_not(is_shift | (j == 0)))
    def _():
        o_ref[...] = _dot(hn_ref[0:tm, :], w_ref[...])


def _in_proj(x2, sc, sh, g1, w_pad, mu_pad, seq):
    m, d = x2.shape
    tm, tn, n_split = 1024, 512, 4
    per_seq = seq // tm
    nblk8 = m // 8
    mod = lambda i, j: (i // per_seq, 0, 0)
    shift_tiles = tuple(range(RKV_COLS // tn)) + ((RKV_COLS + CONV_COLS) // tn,)
    return pl.pallas_call(
        functools.partial(_inproj_kernel, seq, n_split, shift_tiles),
        grid=(m // tm, H_COLS // tn),
        in_specs=[pl.BlockSpec((tm, d), lambda i, j: (i, 0)),
                  pl.BlockSpec((8, d), lambda i, j: (jnp.maximum(i * (tm // 8) - 1, 0), 0)),
                  pl.BlockSpec((8, d), lambda i, j: (jnp.minimum((i + 1) * (tm // 8), nblk8 - 1), 0)),
                  pl.BlockSpec((1, 1, d), mod),
                  pl.BlockSpec((1, 1, d), mod),
                  pl.BlockSpec((1, d), lambda i, j: (0, 0)),
                  pl.BlockSpec((d, tn), lambda i, j: (0, j)),
                  pl.BlockSpec((1, tn), lambda i, j: (0, j))],
        out_specs=pl.BlockSpec((tm, tn), lambda i, j: (i, j)),
        out_shape=jax.ShapeDtypeStruct((m, H_COLS), F32),
        scratch_shapes=[pltpu.VMEM((tm + 16, d), BF16)],
        compiler_params=_cparams(2),
        name="in_proj",
    )(x2, x2, x2, sc, sh, g1, w_pad, mu_pad)


def _prep_kernel(rkv_ref, sm_ref, wfb_ref, wa_ref, wg_ref,
                 w0_ref, a0_ref, kk_ref, ka_ref, rk_ref,
                 r_out, k_out, v_out, kk_out, b_out, lwf_out, lwb_out, g_out, bonus_out):
    xs_sm = sm_ref[...]
    lora_fb = _dot(jnp.tanh(xs_sm[:, 0:LANES]).astype(BF16), wfb_ref[...])
    lora_a = _dot(xs_sm[:, LANES:2 * LANES].astype(BF16), wa_ref[...])
    g = _dot(jax.nn.sigmoid(xs_sm[:, LANES:]).astype(BF16), wg_ref[...])

    decay_scale = math.exp(-0.5)
    w0 = w0_ref[...]
    lwf_out[...] = -decay_scale * jax.nn.sigmoid(w0[0:1] + lora_fb[:, :RWKV_WIDTH])
    lwb_out[...] = -decay_scale * jax.nn.sigmoid(w0[1:2] + lora_fb[:, RWKV_WIDTH:])
    g_out[...] = g.astype(g_out.dtype)
    a = jax.nn.sigmoid(a0_ref[...] + lora_a)

    xk = rkv_ref[:, RWKV_WIDTH:2 * RWKV_WIDTH]
    kk_raw = xk * kk_ref[...]
    ss = _head_sum_bcast(kk_raw * kk_raw, two_pass=True)
    kk = kk_raw / jnp.maximum(jnp.sqrt(ss), 1e-12)
    k = xk * (1.0 + (a - 1.0) * ka_ref[...])
    kk_out[...] = kk.astype(kk_out.dtype)
    b_out[...] = (kk * a).astype(b_out.dtype)
    k_out[...] = k.astype(k_out.dtype)
    r = rkv_ref[:, 0:RWKV_WIDTH]
    r_out[...] = r.astype(r_out.dtype)
    rk_sum = _head_sum_bcast(r * k * rk_ref[...], two_pass=True)
    v = rkv_ref[:, 2 * RWKV_WIDTH:]
    v_out[...] = v.astype(v_out.dtype)
    bonus_out[...] = (rk_sum * v).astype(bonus_out.dtype)


def _prep(h, wfb, wa, wg, w0, a0, k_k, k_a, r_k):
    m = h.shape[0]
    tp = 256
    small_blk = (RKV_COLS + CONV_COLS) // SMALL_PAD
    row = lambda i: (i, 0)
    const = lambda i: (0, 0)
    out_dtypes = [BF16] * 5 + [F32, F32, BF16, BF16]
    return pl.pallas_call(
        _prep_kernel,
        grid=(m // tp,),
        in_specs=[pl.BlockSpec((tp, RKV_COLS), row),
                  pl.BlockSpec((tp, SMALL_PAD), lambda i: (i, small_blk)),
                  pl.BlockSpec(wfb.shape, const),
                  pl.BlockSpec(wa.shape, const),
                  pl.BlockSpec(wg.shape, const),
                  pl.BlockSpec((2, RWKV_WIDTH), const),
                  pl.BlockSpec((1, RWKV_WIDTH), const),
                  pl.BlockSpec((1, RWKV_WIDTH), const),
                  pl.BlockSpec((1, RWKV_WIDTH), const),
                  pl.BlockSpec((1, RWKV_WIDTH), const)],
        out_specs=[pl.BlockSpec((tp, RWKV_WIDTH), row)] * 9,
        out_shape=[jax.ShapeDtypeStruct((m, RWKV_WIDTH), dt) for dt in out_dtypes],
        compiler_params=_cparams(1),
        name="prep",
    )(h, h, wfb, wa, wg, w0, a0, k_k, k_a, r_k)


def _block_diag(x, lane_lo):
    return jnp.concatenate([jnp.where(lane_lo, x, 0.0).astype(BF16),
                            jnp.where(lane_lo, 0.0, x).astype(BF16)], axis=0)


def _chunk_step(chains, common):
    eye2, lane_lo, bd_mask = common
    bd = lambda x: _block_diag(x, lane_lo)
    n = len(chains)
    at, rt, kt, bt, g_all = [], [], [], [], []
    for (r, k, v, kk, b, lw, cum, _, _, bwd) in chains:
        gq = jnp.exp(cum)
        gk = jnp.exp(-cum)
        g_all.append(gq[0:1] if bwd else gq[CHUNK - 1:CHUNK])
        at.append(-(kk * jnp.exp(cum - lw)))
        rt.append(r * gq)
        kt.append(k * gk)
        bt.append(b * gk)

    a_ak, a_ab, a_rk, a_rb = [], [], [], []
    for c in range(n):
        tri_strict, tri_incl = chains[c][8]
        lhs = jnp.concatenate([at[c], rt[c]], axis=0).astype(BF16)
        rhs = jnp.concatenate([bd(kt[c]), bd(bt[c])], axis=0)
        big = _dot_nt(lhs, rhs)
        a_ak.append(jnp.where(tri_strict, big[0:CHUNK, 0:LANES], 0.0))
        a_ab.append(jnp.where(tri_strict, big[0:CHUNK, LANES:], 0.0))
        a_rk.append(jnp.where(tri_incl, big[CHUNK:, 0:LANES], 0.0))
        a_rb.append(jnp.where(tri_incl, big[CHUNK:, LANES:], 0.0))

    q = [eye2 + a for a in a_ab]
    p = [_dot(a.astype(BF16), bd(a)) for a in a_ab]
    xy = [_dot(jnp.concatenate([a_ak[c], a_rk[c]], axis=0).astype(BF16), bd(chains[c][2])) for c in range(n)]
    for _ in range(4):
        pq_pp = [_dot(p[c].astype(BF16), jnp.concatenate([bd(q[c]), bd(p[c])], axis=1)) for c in range(n)]
        q = [q[c] + pq_pp[c][:, 0:LANES] for c in range(n)]
        p = [pq_pp[c][:, LANES:] for c in range(n)]
    q = [q[c] + _dot(p[c].astype(BF16), bd(q[c])) for c in range(n)]

    wu = [_dot(q[c].astype(BF16), jnp.concatenate([bd(at[c]), bd(xy[c][0:CHUNK])], axis=1)) for c in range(n)]
    s1 = [_dot_nt(jnp.concatenate([wu[c][:, 0:LANES], rt[c]], axis=0).astype(BF16), chains[c][7].astype(BF16))
          for c in range(n)]
    u = [wu[c][:, LANES:] + s1[c][0:CHUNK] for c in range(n)]
    ys = [s1[c][CHUNK:] + xy[c][CHUNK:] + _dot(a_rb[c].astype(BF16), bd(u[c])) for c in range(n)]
    states = []
    for c in range(n):
        vu_t = jnp.concatenate([chains[c][2], u[c]], axis=0).T.astype(BF16)
        kb = jnp.concatenate([kt[c], bt[c]], axis=0).astype(BF16)
        states.append(jnp.where(bd_mask, (chains[c][7] + _dot(vu_t, kb)) * g_all[c], 0.0))
    return ys, states


def _scan_kernel(n_chunks,
                 rf_ref, kf_ref, vf_ref, kkf_ref, bf_ref, lwf_ref,
                 rb_ref, kb_ref, vb_ref, kkb_ref, bb_ref, lwb_ref,
                 yf_ref, yb_ref, state_ref, cumf_ref, cumb_ref):
    @pl.when(pl.program_id(1) == 0)
    def _():
        state_ref[...] = jnp.zeros_like(state_ref)

    n_pairs = rf_ref.shape[1] // LANES
    tb = n_chunks * CHUNK
    ri = lax.broadcasted_iota(jnp.int32, (CHUNK, LANES), 0)
    ci = lax.broadcasted_iota(jnp.int32, (CHUNK, LANES), 1) & (CHUNK - 1)
    lane_lo = lax.broadcasted_iota(jnp.int32, (CHUNK, LANES), 1) < CHUNK
    eye2 = jnp.where(ri == ci, 1.0, 0.0)
    br = lax.broadcasted_iota(jnp.int32, (LANES, LANES), 0) >> 6
    bc = lax.broadcasted_iota(jnp.int32, (LANES, LANES), 1) >> 6
    common = (eye2, lane_lo, br == bc)
    tri_f = (ci < ri, ci <= ri)
    tri_b = (ci > ri, ci >= ri)

    tr = lax.broadcasted_iota(jnp.int32, (tb, tb), 0)
    tc = lax.broadcasted_iota(jnp.int32, (tb, tb), 1)
    same = (tr >> 6) == (tc >> 6)
    ones_f = jnp.where(same & (tc <= tr), 1.0, 0.0).astype(BF16)
    ones_b = jnp.where(same & (tc >= tr), 1.0, 0.0).astype(BF16)

    def cumsum(ones, lw):
        hi, lo = _split_bf16(lw)
        return _dot(ones, hi) + _dot(ones, lo)

    cumf_ref[...] = cumsum(ones_f, lwf_ref[...])
    cumb_ref[...] = cumsum(ones_b, lwb_ref[...])

    for step in range(n_chunks):
        rows_f = slice(step * CHUNK, (step + 1) * CHUNK)
        rows_b = slice((n_chunks - 1 - step) * CHUNK, (n_chunks - step) * CHUNK)
        chains = []
        for p in range(n_pairs):
            cols = slice(p * LANES, (p + 1) * LANES)
            tok_f = [ref[rows_f, cols].astype(F32) for ref in (rf_ref, kf_ref, vf_ref, kkf_ref, bf_ref)]
            tok_b = [ref[rows_b, cols].astype(F32) for ref in (rb_ref, kb_ref, vb_ref, kkb_ref, bb_ref)]
            chains.append((*tok_f, lwf_ref[rows_f, cols], cumf_ref[rows_f, cols], state_ref[0, p], tri_f, False))
            chains.append((*tok_b, lwb_ref[rows_b, cols], cumb_ref[rows_b, cols], state_ref[1, p], tri_b, True))
        ys, states = _chunk_step(chains, common)
        for p in range(n_pairs):
            cols = slice(p * LANES, (p + 1) * LANES)
            yf_ref[rows_f, cols] = ys[2 * p].astype(yf_ref.dtype)
            yb_ref[rows_b, cols] = ys[2 * p + 1].astype(yb_ref.dtype)
            state_ref[0, p] = states[2 * p]
            state_ref[1, p] = states[2 * p + 1]


def _scan(r, k, v, kk, b, lwf, lwb, seq):
    m = r.shape[0]
    n_chunks = 2
    tb = n_chunks * CHUNK
    nb = seq // tb
    fwd = lambda s, i: (s * nb + i, 0)
    bwd = lambda s, i: (s * nb + nb - 1 - i, 0)
    blk = lambda im: pl.BlockSpec((tb, RWKV_WIDTH), im)
    out = jax.ShapeDtypeStruct((m, RWKV_WIDTH), BF16)
    return pl.pallas_call(
        functools.partial(_scan_kernel, n_chunks),
        grid=(m // seq, nb),
        in_specs=[blk(fwd)] * 6 + [blk(bwd)] * 6,
        out_specs=[blk(fwd), blk(bwd)],
        out_shape=[out, out],
        scratch_shapes=[pltpu.VMEM((2, RWKV_WIDTH // LANES, LANES, LANES), F32),
                        pltpu.VMEM((tb, RWKV_WIDTH), F32),
                        pltpu.VMEM((tb, RWKV_WIDTH), F32)],
        compiler_params=_cparams(2),
        name="scan",
    )(r, k, v, kk, b, lwf, r, k, v, kk, b, lwb)


def _post_kernel(seq,
                 yf_ref, yb_ref, g_ref, bonus_ref, cv_ref, cvp_ref, cvn_ref,
                 lng_ref, lnb_ref, cw_ref, x_ref, gt_ref, w_ref, o_ref, mix_ref):
    tm = yf_ref.shape[0]
    t0 = (pl.program_id(0) * tm) % seq
    first = t0 == 0
    last = t0 + tm == seq
    y = yf_ref[...].astype(F32) + yb_ref[...].astype(F32)
    mean = _head_sum_bcast(y, two_pass=True) * (1.0 / HEAD)
    d = y - mean
    var = _head_sum_bcast(d * d, two_pass=True) * (1.0 / HEAD)
    yn = d * lax.rsqrt(var + LNX_EPS) * lng_ref[...] + lnb_ref[...]
    mix_ref[:, 0:RWKV_WIDTH] = ((yn + bonus_ref[...].astype(F32)) * g_ref[...].astype(F32)).astype(BF16)

    c1, c2 = slice(CONV_WIDTH, 2 * CONV_WIDTH), slice(2 * CONV_WIDTH, 3 * CONV_WIDTH)
    z = cv_ref[:, c1] * cv_ref[:, c2]
    zp = jnp.where(first, 0.0, cvp_ref[7:8, c1] * cvp_ref[7:8, c2])
    zn = jnp.where(last, 0.0, cvn_ref[0:1, c1] * cvn_ref[0:1, c2])
    ridx = lax.broadcasted_iota(jnp.int32, z.shape, 0)
    z_prev = jnp.where(ridx == 0, zp, pltpu.roll(z, 1, axis=0))
    z_next = jnp.where(ridx == tm - 1, zn, pltpu.roll(z, tm - 1, axis=0))
    cw = cw_ref[...]
    conv = z_prev * cw[0:1] + z * cw[1:2] + z_next * cw[2:3]
    mix_ref[:, RWKV_WIDTH:] = (cv_ref[:, 0:CONV_WIDTH] * conv).astype(BF16)

    gt = gt_ref[0]
    for c0 in range(0, o_ref.shape[1], POST_OUT_COLS):
        cols = slice(c0, c0 + POST_OUT_COLS)
        o_ref[:, cols] = x_ref[:, cols] + gt[:, cols] * _dot(mix_ref[...], w_ref[:, cols])


def _post(yf, yb, g, bonus, h, lnx_g, lnx_b, conv_w, x2, gt1, w_out, seq):
    m, d = x2.shape
    tm = 256
    per_seq = seq // tm
    nblk8 = m // 8
    row = lambda i: (i, 0)
    const = lambda i: (0, 0)
    return pl.pallas_call(
        functools.partial(_post_kernel, seq),
        grid=(m // tm,),
        in_specs=[pl.BlockSpec((tm, RWKV_WIDTH), row)] * 4 + [
            pl.BlockSpec((tm, CONV_COLS), lambda i: (i, 1)),
            pl.BlockSpec((8, CONV_COLS), lambda i: (jnp.maximum(i * (tm // 8) - 1, 0), 1)),
            pl.BlockSpec((8, CONV_COLS), lambda i: (jnp.minimum((i + 1) * (tm // 8), nblk8 - 1), 1)),
            pl.BlockSpec((1, RWKV_WIDTH), const),
            pl.BlockSpec((1, RWKV_WIDTH), const),
            pl.BlockSpec((3, CONV_WIDTH), const),
            pl.BlockSpec((tm, d), row),
            pl.BlockSpec((1, 1, d), lambda i: (i // per_seq, 0, 0)),
            pl.BlockSpec((d, d), const, pipeline_mode=pl.Buffered(1))],
        out_specs=pl.BlockSpec((tm, d), row),
        out_shape=jax.ShapeDtypeStruct((m, d), F32),
        scratch_shapes=[pltpu.VMEM((tm, d), BF16)],
        compiler_params=_cparams(1),
        name="post",
    )(yf, yb, g, bonus, h, h, h, lnx_g, lnx_b, conv_w, x2, gt1, w_out)


def _ffn_kernel(n_split, x_ref, sc_ref, sh_ref, gt_ref, g2_ref, gf_ref, wg_ref, wu_ref, wd_ref, o_ref, hn_ref):
    j = pl.program_id(1)
    last = pl.num_programs(1) - 1
    tm, n_out = o_ref.shape
    rows = tm // n_split

    def swiglu_step(sl, init):
        hn = hn_ref[sl, :]
        gate = _dot(hn, wg_ref[...])
        up = _dot(hn, wu_ref[...])
        act = (gate * jax.nn.sigmoid(gate) * up).astype(BF16)
        for c0 in range(0, n_out, FFN_DOWN_COLS):
            cols = slice(c0, c0 + FFN_DOWN_COLS)
            part = _dot(act, wd_ref[:, cols])
            if init:
                o_ref[sl, cols] = part
            else:
                o_ref[sl, cols] += part

    @pl.when(j == 0)
    def _():
        for q in range(n_split):
            sl = slice(q * rows, (q + 1) * rows)
            x = x_ref[sl, :]
            ms = jnp.mean(x * x, axis=-1, keepdims=True)
            y = x * lax.rsqrt(ms + RMS_EPS) * g2_ref[...]
            hn_ref[sl, :] = (y * (1.0 + sc_ref[0]) + sh_ref[0]).astype(BF16)
            swiglu_step(sl, True)

    @pl.when((j > 0) & (j < last))
    def _():
        swiglu_step(slice(0, tm), False)

    @pl.when(j == last)
    def _():
        for q in range(n_split):
            sl = slice(q * rows, (q + 1) * rows)
            swiglu_step(sl, False)
            x = x_ref[sl, :] + gt_ref[0] * o_ref[sl, :]
            ms = jnp.mean(x * x, axis=-1, keepdims=True)
            o_ref[sl, :] = x * lax.rsqrt(ms + RMS_EPS) * gf_ref[...]


def _ffn(x1, sc, sh, gt, g2, gf, wg, wu, wd, seq):
    m, d = x1.shape
    tm, tf, n_split = 1024, 512, 4
    per_seq = seq // tm
    row = lambda i, j: (i, 0)
    mod = lambda i, j: (i // per_seq, 0, 0)
    const = lambda i, j: (0, 0)
    return pl.pallas_call(
        functools.partial(_ffn_kernel, n_split),
        grid=(m // tm, D_FF // tf),
        in_specs=[pl.BlockSpec((tm, d), row),
                  pl.BlockSpec((1, 1, d), mod),
                  pl.BlockSpec((1, 1, d), mod),
                  pl.BlockSpec((1, 1, d), mod),
                  pl.BlockSpec((1, d), const),
                  pl.BlockSpec((1, d), const),
                  pl.BlockSpec((d, tf), lambda i, j: (0, j)),
                  pl.BlockSpec((d, tf), lambda i, j: (0, j)),
                  pl.BlockSpec((tf, d), lambda i, j: (j, 0))],
        out_specs=pl.BlockSpec((tm, d), row),
        out_shape=jax.ShapeDtypeStruct((m, d), F32),
        scratch_shapes=[pltpu.VMEM((tm, d), BF16)],
        compiler_params=_cparams(2),
        name="ffn",
    )(x1, sc, sh, gt, g2, gf, wg, wu, wd)


def _layer_weights(w_in, mu_shift, w2_decay, a2, g2_gate):
    d = w_in.shape[0]
    pad = H_COLS - (RKV_COLS + CONV_COLS + SMALL_COLS)
    w_pad = jnp.concatenate([w_in[:, :RKV_COLS], w_in[:, RKV_COLS + SMALL_COLS:],
                             w_in[:, RKV_COLS:RKV_COLS + SMALL_COLS], jnp.zeros((d, pad), F32)],
                            axis=1).astype(BF16)
    mu_pad = jnp.concatenate([mu_shift[:RKV_COLS], jnp.zeros((CONV_COLS,), F32), mu_shift[RKV_COLS:],
                              jnp.zeros((pad,), F32)]).reshape(1, H_COLS)
    zeros = lambda rows, cols: jnp.zeros((rows, cols), F32)
    wfb = jnp.concatenate([
        jnp.concatenate([w2_decay[0], zeros(DECAY_RANK, RWKV_WIDTH)], axis=1),
        jnp.concatenate([zeros(DECAY_RANK, RWKV_WIDTH), w2_decay[1]], axis=1)], axis=0).astype(BF16)
    wa = jnp.concatenate([a2, zeros(LANES - AAA_RANK, RWKV_WIDTH)], axis=0).astype(BF16)
    g_rows = SMALL_PAD - LANES
    wg = jnp.concatenate([zeros(AAA_RANK, RWKV_WIDTH), g2_gate,
                          zeros(g_rows - AAA_RANK - GATE_RANK, RWKV_WIDTH)], axis=0).astype(BF16)
    return w_pad, mu_pad, wfb, wa, wg


def _group(x, mod, lw):
    nb, seq, d = x.shape
    x2 = x.reshape(nb * seq, d)
    sh1, sc1, gt1, sh2, sc2, gt2 = [mod[:, q * d:(q + 1) * d].reshape(nb, 1, d) for q in range(6)]
    h = _in_proj(x2, sc1, sh1, lw["norm1_g"], lw["w_pad"], lw["mu_pad"], seq)
    r, k, v, kk, b, lwf, lwb, g, bonus = _prep(
        h, lw["wfb"], lw["wa"], lw["wg"], lw["w0"], lw["a0"], lw["k_k"], lw["k_a"], lw["r_k"])
    yf, yb = _scan(r, k, v, kk, b, lwf, lwb, seq)
    x1 = _post(yf, yb, g, bonus, h, lw["lnx_g"], lw["lnx_b"], lw["conv_w"], x2, gt1, lw["w_out"], seq)
    out = _ffn(x1, sc2, sh2, gt2, lw["norm2_g"], lw["norm_f_g"], lw["wg_ffn"], lw["wu_ffn"], lw["wd_ffn"], seq)
    return out.reshape(nb, seq, d)


def kernel(x_prompt, x_sample, c_prompt, c_sample, w_ada, b_ada, norm1_g, w_in, mu_shift, w0_decay, w2_decay, a0, a2, g2_gate, k_k, k_a, r_k, lnx_g, lnx_b, conv_w, w_out, norm2_g, w_ffn_gate, w_ffn_up, w_ffn_down, norm_f_g):
    assert w_ada.shape[0] == 1, "single-layer block"
    l = 0
    row = lambda t: t.reshape(1, -1)
    w_pad, mu_pad, wfb, wa, wg = _layer_weights(w_in[l], mu_shift[l], w2_decay[l], a2[l], g2_gate[l])
    lw = dict(w_pad=w_pad, mu_pad=mu_pad, wfb=wfb, wa=wa, wg=wg,
              norm1_g=row(norm1_g[l]), w0=w0_decay[l], a0=row(a0[l]), k_k=row(k_k[l]), k_a=row(k_a[l]),
              r_k=row(r_k[l]), lnx_g=row(lnx_g[l]), lnx_b=row(lnx_b[l]), conv_w=conv_w[l],
              w_out=w_out[l].astype(BF16), norm2_g=row(norm2_g[l]), norm_f_g=row(norm_f_g),
              wg_ffn=w_ffn_gate[l].astype(BF16), wu_ffn=w_ffn_up[l].astype(BF16),
              wd_ffn=w_ffn_down[l].astype(BF16))
    n_prompt = c_prompt.shape[0]
    mod = _mod(jnp.concatenate([c_prompt, c_sample], axis=0), w_ada[l], b_ada[l])
    y_prompt = _group(x_prompt, mod[:n_prompt], lw)
    y_sample = _group(x_sample, mod[n_prompt:], lw)
    return (y_prompt, y_sample)
```

```python
import functools
import math

import jax
import jax.numpy as jnp
from jax import lax
from jax.experimental import pallas as pl
from jax.experimental.pallas import tpu as pltpu

F32 = jnp.float32
BF16 = jnp.bfloat16

D_MODEL = 2048
RWKV_WIDTH = 1024
CONV_WIDTH = 1024
HEAD = 64
DECAY_RANK = 64
AAA_RANK = 64
GATE_RANK = 160
SMALL_COLS = 2 * DECAY_RANK + AAA_RANK + GATE_RANK
SMALL_PAD = 512
RKV_COLS = 3 * RWKV_WIDTH
CONV_COLS = 3 * CONV_WIDTH
H_COLS = RKV_COLS + CONV_COLS + SMALL_PAD
D_FF = 5632
RMS_EPS = 1e-6
LNX_EPS = 64e-5

CHUNK = 64
LANES = 128
MXU_COLS = 256
HALO = 16
VMEM_LIMIT = 56 * 1024 * 1024
FFN_DOWN_COLS = 512
POST_OUT_COLS = 512


def _cparams(n_axes):
    return pltpu.CompilerParams(dimension_semantics=("arbitrary",) * n_axes,
                                vmem_limit_bytes=VMEM_LIMIT)


def _dot(a, b):
    return jnp.dot(a, b, preferred_element_type=F32)


def _dot_nt(a, b):
    return lax.dot_general(a, b, (((1,), (1,)), ((), ())), preferred_element_type=F32)


def _split_bf16(x):
    hi = x.astype(BF16)
    lo = (x - hi.astype(F32)).astype(BF16)
    return hi, lo


def _head_sum_bcast(x):
    n = MXU_COLS
    row = lax.broadcasted_iota(jnp.int32, (n, n), 0) >> 6
    col = lax.broadcasted_iota(jnp.int32, (n, n), 1) >> 6
    ones_bd = jnp.where(row == col, 1.0, 0.0).astype(BF16)
    outs = [_dot(x[:, q * n:(q + 1) * n].astype(BF16), ones_bd) for q in range(x.shape[1] // n)]
    return jnp.concatenate(outs, axis=1)


def _mod_kernel(c_ref, w_ref, b_ref, o_ref):
    c = c_ref[...]
    s = c * jax.nn.sigmoid(c)
    o_ref[...] = _dot(s.astype(BF16), w_ref[...].astype(BF16)) + b_ref[...]


def _mod(c, w_ada, b_ada):
    nb, d = c.shape
    n = w_ada.shape[1]
    tn = 1024
    return pl.pallas_call(
        _mod_kernel,
        grid=(n // tn,),
        in_specs=[pl.BlockSpec((nb, d), lambda j: (0, 0)),
                  pl.BlockSpec((d, tn), lambda j: (0, j)),
                  pl.BlockSpec((1, tn), lambda j: (0, j))],
        out_specs=pl.BlockSpec((nb, tn), lambda j: (0, j)),
        out_shape=jax.ShapeDtypeStruct((nb, n), F32),
        compiler_params=_cparams(1),
        name="mod",
    )(c, w_ada, b_ada.reshape(1, n))


def _token_shift_mix(h, prev_row, next_row, mu):
    rows, cols = h.shape
    r8 = lax.broadcasted_iota(jnp.int32, (8, cols), 0)
    even = (lax.broadcasted_iota(jnp.int32, (rows, cols), 1) & 1) == 0
    prv = pltpu.roll(h, 1, axis=0)
    prv = jnp.concatenate([jnp.where(r8 == 0, prev_row, prv[0:8]), prv[8:]], axis=0)
    nxt = pltpu.roll(h, rows - 1, axis=0)
    nxt = jnp.concatenate([nxt[:rows - 8], jnp.where(r8 == 7, next_row, nxt[rows - 8:])], axis=0)
    return h + mu * (jnp.where(even, prv, nxt) - h)


def _inproj_kernel(seq, n_split, shift_tiles, x_ref, xp_ref, xn_ref, sc_ref, sh_ref, g_ref, w_ref, mu_ref,
                   o_ref, hn_ref):
    tm = x_ref.shape[0]
    tn = o_ref.shape[1]
    t0 = (pl.program_id(0) * tm) % seq
    j = pl.program_id(1)
    is_shift = functools.reduce(lambda a, b: a | b, [j == t for t in shift_tiles if t != 0])

    def normed(x):
        ms = jnp.mean(x * x, axis=-1, keepdims=True)
        y = x * lax.rsqrt(ms + RMS_EPS) * g_ref[...]
        return y * (1.0 + sc_ref[0]) + sh_ref[0]

    def mix_store(h_all, cols):
        prev_row = jnp.where(t0 == 0, 0.0, h_all[tm + 7:tm + 8])
        next_row = jnp.where(t0 + tm == seq, 0.0, h_all[tm + 8:tm + 9])
        o_ref[:, cols] = _token_shift_mix(h_all[0:tm], prev_row, next_row, mu_ref[:, cols]).astype(o_ref.dtype)

    @pl.when(j == 0)
    def _():
        hn_ref[tm:, :] = jnp.concatenate([normed(xp_ref[...]), normed(xn_ref[...])], axis=0).astype(BF16)
        rows = tm // n_split
        parts = []
        for q in range(n_split):
            sl = slice(q * rows, (q + 1) * rows)
            hn_ref[sl, :] = normed(x_ref[sl, :]).astype(BF16)
            sl = slice(q * rows, (q + 1) * rows if q + 1 < n_split else tm + 16)
            parts.append(_dot(hn_ref[sl, :], w_ref[...]))
        h_all = jnp.concatenate(parts, axis=0)
        for c0 in range(0, tn, MXU_COLS):
            mix_store(h_all[:, c0:c0 + MXU_COLS], slice(c0, c0 + MXU_COLS))

    @pl.when(is_shift)
    def _():
        for c0 in range(0, tn, MXU_COLS):
            cols = slice(c0, c0 + MXU_COLS)
            mix_store(_dot(hn_ref[...], w_ref[:, cols]), cols)

    @pl.when(jnp.logical_not(is_shift | (j == 0)))
    def _():
        o_ref[...] = _dot(hn_ref[0:tm, :], w_ref[...]).astype(o_ref.dtype)


def _in_proj(x2, sc, sh, g1, w_pad, mu_pad, seq):
    m, d = x2.shape
    tm, tn, n_split = 1024, 512, 4
    per_seq = seq // tm
    nblk8 = m // 8
    mod = lambda i, j: (i // per_seq, 0, 0)
    shift_tiles = tuple(range(RKV_COLS // tn)) + ((RKV_COLS + CONV_COLS) // tn,)
    return pl.pallas_call(
        functools.partial(_inproj_kernel, seq, n_split, shift_tiles),
        grid=(m // tm, H_COLS // tn),
        in_specs=[pl.BlockSpec((tm, d), lambda i, j: (i, 0)),
                  pl.BlockSpec((8, d), lambda i, j: (jnp.maximum(i * (tm // 8) - 1, 0), 0)),
                  pl.BlockSpec((8, d), lambda i, j: (jnp.minimum((i + 1) * (tm // 8), nblk8 - 1), 0)),
                  pl.BlockSpec((1, 1, d), mod),
                  pl.BlockSpec((1, 1, d), mod),
                  pl.BlockSpec((1, d), lambda i, j: (0, 0)),
                  pl.BlockSpec((d, tn), lambda i, j: (0, j)),
                  pl.BlockSpec((1, tn), lambda i, j: (0, j))],
        out_specs=pl.BlockSpec((tm, tn), lambda i, j: (i, j)),
        out_shape=jax.ShapeDtypeStruct((m, H_COLS), BF16),
        scratch_shapes=[pltpu.VMEM((tm + 16, d), BF16)],
        compiler_params=_cparams(2),
        name="in_proj",
    )(x2, x2, x2, sc, sh, g1, w_pad, mu_pad)


def _prep_kernel(rkv_ref, sm_ref, wfb_ref, wa_ref, wg_ref,
                 w0_ref, a0_ref, kk_ref, ka_ref, rk_ref,
                 k_out, kk_out, b_out, lwf_out, lwb_out, g_out, bonus_out):
    xs_sm = sm_ref[...].astype(F32)
    lora_fb = _dot(jnp.tanh(xs_sm[:, 0:LANES]).astype(BF16), wfb_ref[...])
    lora_a = _dot(xs_sm[:, LANES:2 * LANES].astype(BF16), wa_ref[...])
    g = _dot(jax.nn.sigmoid(xs_sm[:, LANES:]).astype(BF16), wg_ref[...])

    decay_scale = math.exp(-0.5)
    w0 = w0_ref[...]
    lwf_out[...] = -decay_scale * jax.nn.sigmoid(w0[0:1] + lora_fb[:, :RWKV_WIDTH])
    lwb_out[...] = -decay_scale * jax.nn.sigmoid(w0[1:2] + lora_fb[:, RWKV_WIDTH:])
    g_out[...] = g.astype(g_out.dtype)
    a = jax.nn.sigmoid(a0_ref[...] + lora_a)

    xk = rkv_ref[:, RWKV_WIDTH:2 * RWKV_WIDTH].astype(F32)
    kk_raw = xk * kk_ref[...]
    ss = _head_sum_bcast(kk_raw * kk_raw)
    kk = kk_raw / jnp.maximum(jnp.sqrt(ss), 1e-12)
    k = xk * (1.0 + (a - 1.0) * ka_ref[...])
    kk_out[...] = kk.astype(kk_out.dtype)
    b_out[...] = (kk * a).astype(b_out.dtype)
    k_out[...] = k.astype(k_out.dtype)
    r = rkv_ref[:, 0:RWKV_WIDTH].astype(F32)
    rk_sum = _head_sum_bcast(r * k * rk_ref[...])
    v = rkv_ref[:, 2 * RWKV_WIDTH:].astype(F32)
    bonus_out[...] = (rk_sum * v).astype(bonus_out.dtype)


def _prep(h, wfb, wa, wg, w0, a0, k_k, k_a, r_k):
    m = h.shape[0]
    tp = 512
    small_blk = (RKV_COLS + CONV_COLS) // SMALL_PAD
    row = lambda i: (i, 0)
    const = lambda i: (0, 0)
    out_dtypes = [BF16] * 3 + [F32, F32, BF16, BF16]
    return pl.pallas_call(
        _prep_kernel,
        grid=(m // tp,),
        in_specs=[pl.BlockSpec((tp, RKV_COLS), row),
                  pl.BlockSpec((tp, SMALL_PAD), lambda i: (i, small_blk)),
                  pl.BlockSpec(wfb.shape, const),
                  pl.BlockSpec(wa.shape, const),
                  pl.BlockSpec(wg.shape, const),
                  pl.BlockSpec((2, RWKV_WIDTH), const),
                  pl.BlockSpec((1, RWKV_WIDTH), const),
                  pl.BlockSpec((1, RWKV_WIDTH), const),
                  pl.BlockSpec((1, RWKV_WIDTH), const),
                  pl.BlockSpec((1, RWKV_WIDTH), const)],
        out_specs=[pl.BlockSpec((tp, RWKV_WIDTH), row)] * len(out_dtypes),
        out_shape=[jax.ShapeDtypeStruct((m, RWKV_WIDTH), dt) for dt in out_dtypes],
        compiler_params=_cparams(1),
        name="prep",
    )(h, h, wfb, wa, wg, w0, a0, k_k, k_a, r_k)


def _block_diag(x, lane_lo):
    return jnp.concatenate([jnp.where(lane_lo, x, 0.0).astype(BF16),
                            jnp.where(lane_lo, 0.0, x).astype(BF16)], axis=0)


def _chunk_step(chains, common):
    eye2, lane_lo, bd_mask = common
    bd = lambda x: _block_diag(x, lane_lo)
    n = len(chains)
    at, rt, kt, bt, g_all = [], [], [], [], []
    for (r, k, v, kk, b, lw, cum, _, _, bwd) in chains:
        gq = jnp.exp(cum)
        gk = jnp.exp(-cum)
        g_all.append(gq[0:1] if bwd else gq[CHUNK - 1:CHUNK])
        at.append(-(kk * jnp.exp(cum - lw)))
        rt.append(r * gq)
        kt.append(k * gk)
        bt.append(b * gk)

    a_ak, a_ab, a_rk, a_rb = [], [], [], []
    for c in range(n):
        tri_strict, tri_incl = chains[c][8]
        lhs = jnp.concatenate([at[c], rt[c]], axis=0).astype(BF16)
        rhs = jnp.concatenate([bd(kt[c]), bd(bt[c])], axis=0)
        big = _dot_nt(lhs, rhs)
        a_ak.append(jnp.where(tri_strict, big[0:CHUNK, 0:LANES], 0.0))
        a_ab.append(jnp.where(tri_strict, big[0:CHUNK, LANES:], 0.0))
        a_rk.append(jnp.where(tri_incl, big[CHUNK:, 0:LANES], 0.0))
        a_rb.append(jnp.where(tri_incl, big[CHUNK:, LANES:], 0.0))

    q = [eye2 + a for a in a_ab]
    p = [_dot(a.astype(BF16), bd(a)) for a in a_ab]
    xy = [_dot(jnp.concatenate([a_ak[c], a_rk[c]], axis=0).astype(BF16), bd(chains[c][2])) for c in range(n)]
    for _ in range(4):
        pq_pp = [_dot(p[c].astype(BF16), jnp.concatenate([bd(q[c]), bd(p[c])], axis=1)) for c in range(n)]
        q = [q[c] + pq_pp[c][:, 0:LANES] for c in range(n)]
        p = [pq_pp[c][:, LANES:] for c in range(n)]
    q = [q[c] + _dot(p[c].astype(BF16), bd(q[c])) for c in range(n)]

    wu = [_dot(q[c].astype(BF16), jnp.concatenate([bd(at[c]), bd(xy[c][0:CHUNK])], axis=1)) for c in range(n)]
    s1 = [_dot_nt(jnp.concatenate([wu[c][:, 0:LANES], rt[c]], axis=0).astype(BF16), chains[c][7].astype(BF16))
          for c in range(n)]
    u = [wu[c][:, LANES:] + s1[c][0:CHUNK] for c in range(n)]
    ys = [s1[c][CHUNK:] + xy[c][CHUNK:] + _dot(a_rb[c].astype(BF16), bd(u[c])) for c in range(n)]
    states = []
    for c in range(n):
        vu_t = jnp.concatenate([chains[c][2], u[c]], axis=0).T.astype(BF16)
        kb = jnp.concatenate([kt[c], bt[c]], axis=0).astype(BF16)
        states.append(jnp.where(bd_mask, (chains[c][7] + _dot(vu_t, kb)) * g_all[c], 0.0))
    return ys, states


def _scan_kernel(n_chunks,
                 rf_ref, kf_ref, vf_ref, kkf_ref, bf_ref, lwf_ref,
                 rb_ref, kb_ref, vb_ref, kkb_ref, bb_ref, lwb_ref,
                 yf_ref, yb_ref, state_ref, cumf_ref, cumb_ref):
    @pl.when(pl.program_id(1) == 0)
    def _():
        state_ref[...] = jnp.zeros_like(state_ref)

    n_pairs = rf_ref.shape[1] // LANES
    tb = n_chunks * CHUNK
    ri = lax.broadcasted_iota(jnp.int32, (CHUNK, LANES), 0)
    ci = lax.broadcasted_iota(jnp.int32, (CHUNK, LANES), 1) & (CHUNK - 1)
    lane_lo = lax.broadcasted_iota(jnp.int32, (CHUNK, LANES), 1) < CHUNK
    eye2 = jnp.where(ri == ci, 1.0, 0.0)
    br = lax.broadcasted_iota(jnp.int32, (LANES, LANES), 0) >> 6
    bc = lax.broadcasted_iota(jnp.int32, (LANES, LANES), 1) >> 6
    common = (eye2, lane_lo, br == bc)
    tri_f = (ci < ri, ci <= ri)
    tri_b = (ci > ri, ci >= ri)

    tr = lax.broadcasted_iota(jnp.int32, (tb, tb), 0)
    tc = lax.broadcasted_iota(jnp.int32, (tb, tb), 1)
    same = (tr >> 6) == (tc >> 6)
    ones_f = jnp.where(same & (tc <= tr), 1.0, 0.0).astype(BF16)
    ones_b = jnp.where(same & (tc >= tr), 1.0, 0.0).astype(BF16)

    def cumsum(ones, lw):
        hi, lo = _split_bf16(lw)
        return _dot(ones, hi) + _dot(ones, lo)

    cumf_ref[...] = cumsum(ones_f, lwf_ref[...])
    cumb_ref[...] = cumsum(ones_b, lwb_ref[...])

    for step in range(n_chunks):
        rows_f = slice(step * CHUNK, (step + 1) * CHUNK)
        rows_b = slice((n_chunks - 1 - step) * CHUNK, (n_chunks - step) * CHUNK)
        chains = []
        for p in range(n_pairs):
            cols = slice(p * LANES, (p + 1) * LANES)
            tok_f = [ref[rows_f, cols].astype(F32) for ref in (rf_ref, kf_ref, vf_ref, kkf_ref, bf_ref)]
            tok_b = [ref[rows_b, cols].astype(F32) for ref in (rb_ref, kb_ref, vb_ref, kkb_ref, bb_ref)]
            chains.append((*tok_f, lwf_ref[rows_f, cols], cumf_ref[rows_f, cols], state_ref[0, p], tri_f, False))
            chains.append((*tok_b, lwb_ref[rows_b, cols], cumb_ref[rows_b, cols], state_ref[1, p], tri_b, True))
        ys, states = _chunk_step(chains, common)
        for p in range(n_pairs):
            cols = slice(p * LANES, (p + 1) * LANES)
            yf_ref[rows_f, cols] = ys[2 * p].astype(yf_ref.dtype)
            yb_ref[rows_b, cols] = ys[2 * p + 1].astype(yb_ref.dtype)
            state_ref[0, p] = states[2 * p]
            state_ref[1, p] = states[2 * p + 1]


def _scan(h, k, kk, b, lwf, lwb, seq):
    m = k.shape[0]
    n_chunks = 2
    tb = n_chunks * CHUNK
    nb = seq // tb
    fwd = lambda s, i: (s * nb + i, 0)
    bwd = lambda s, i: (s * nb + nb - 1 - i, 0)
    blk = lambda im, col=0: pl.BlockSpec((tb, RWKV_WIDTH), lambda s, i: (im(s, i)[0], col))
    out = jax.ShapeDtypeStruct((m, RWKV_WIDTH), BF16)
    specs = lambda im: [blk(im, 0), blk(im), blk(im, 2), blk(im), blk(im), blk(im)]
    return pl.pallas_call(
        functools.partial(_scan_kernel, n_chunks),
        grid=(m // seq, nb),
        in_specs=specs(fwd) + specs(bwd),
        out_specs=[blk(fwd), blk(bwd)],
        out_shape=[out, out],
        scratch_shapes=[pltpu.VMEM((2, RWKV_WIDTH // LANES, LANES, LANES), F32),
                        pltpu.VMEM((tb, RWKV_WIDTH), F32),
                        pltpu.VMEM((tb, RWKV_WIDTH), F32)],
        compiler_params=_cparams(2),
        name="scan",
    )(h, k, h, kk, b, lwf, h, k, h, kk, b, lwb)


def _post_kernel(seq,
                 yf_ref, yb_ref, g_ref, bonus_ref, cv_ref, cvp_ref, cvn_ref,
                 lng_ref, lnb_ref, cw_ref, x_ref, gt_ref, w_ref, o_ref, mix_ref):
    tm = yf_ref.shape[0]
    t0 = (pl.program_id(0) * tm) % seq
    first = t0 == 0
    last = t0 + tm == seq
    y = yf_ref[...].astype(F32) + yb_ref[...].astype(F32)
    mean = _head_sum_bcast(y) * (1.0 / HEAD)
    d = y - mean
    var = _head_sum_bcast(d * d) * (1.0 / HEAD)
    yn = d * lax.rsqrt(var + LNX_EPS) * lng_ref[...] + lnb_ref[...]
    mix_ref[:, 0:RWKV_WIDTH] = ((yn + bonus_ref[...].astype(F32)) * g_ref[...].astype(F32)).astype(BF16)

    c1, c2 = slice(CONV_WIDTH, 2 * CONV_WIDTH), slice(2 * CONV_WIDTH, 3 * CONV_WIDTH)
    f32 = lambda t: t.astype(F32)
    z = f32(cv_ref[:, c1]) * f32(cv_ref[:, c2])
    hp = f32(cvp_ref[...])[HALO - 1:HALO]
    hn = f32(cvn_ref[...])[0:1]
    zp = jnp.where(first, 0.0, hp[:, c1] * hp[:, c2])
    zn = jnp.where(last, 0.0, hn[:, c1] * hn[:, c2])
    ridx = lax.broadcasted_iota(jnp.int32, z.shape, 0)
    z_prev = jnp.where(ridx == 0, zp, pltpu.roll(z, 1, axis=0))
    z_next = jnp.where(ridx == tm - 1, zn, pltpu.roll(z, tm - 1, axis=0))
    cw = cw_ref[...]
    conv = z_prev * cw[0:1] + z * cw[1:2] + z_next * cw[2:3]
    mix_ref[:, RWKV_WIDTH:] = (f32(cv_ref[:, 0:CONV_WIDTH]) * conv).astype(BF16)

    gt = gt_ref[0]
    for c0 in range(0, o_ref.shape[1], POST_OUT_COLS):
        cols = slice(c0, c0 + POST_OUT_COLS)
        o_ref[:, cols] = x_ref[:, cols] + gt[:, cols] * _dot(mix_ref[...], w_ref[:, cols])


def _post(yf, yb, g, bonus, h, lnx_g, lnx_b, conv_w, x2, gt1, w_out, seq):
    m, d = x2.shape
    tm = 512
    per_seq = seq // tm
    n_halo = m // HALO
    row = lambda i: (i, 0)
    const = lambda i: (0, 0)
    return pl.pallas_call(
        functools.partial(_post_kernel, seq),
        grid=(m // tm,),
        in_specs=[pl.BlockSpec((tm, RWKV_WIDTH), row)] * 4 + [
            pl.BlockSpec((tm, CONV_COLS), lambda i: (i, 1)),
            pl.BlockSpec((HALO, CONV_COLS), lambda i: (jnp.maximum(i * (tm // HALO) - 1, 0), 1)),
            pl.BlockSpec((HALO, CONV_COLS), lambda i: (jnp.minimum((i + 1) * (tm // HALO), n_halo - 1), 1)),
            pl.BlockSpec((1, RWKV_WIDTH), const),
            pl.BlockSpec((1, RWKV_WIDTH), const),
            pl.BlockSpec((3, CONV_WIDTH), const),
            pl.BlockSpec((tm, d), row),
            pl.BlockSpec((1, 1, d), lambda i: (i // per_seq, 0, 0)),
            pl.BlockSpec((d, d), const, pipeline_mode=pl.Buffered(1))],
        out_specs=pl.BlockSpec((tm, d), row),
        out_shape=jax.ShapeDtypeStruct((m, d), F32),
        scratch_shapes=[pltpu.VMEM((tm, d), BF16)],
        compiler_params=_cparams(1),
        name="post",
    )(yf, yb, g, bonus, h, h, h, lnx_g, lnx_b, conv_w, x2, gt1, w_out)


def _ffn_kernel(n_split, x_ref, sc_ref, sh_ref, gt_ref, g2_ref, gf_ref, wg_ref, wu_ref, wd_ref, o_ref, hn_ref):
    j = pl.program_id(1)
    last = pl.num_programs(1) - 1
    tm, n_out = o_ref.shape
    rows = tm // n_split

    def swiglu_step(sl, init):
        hn = hn_ref[sl, :]
        gate = _dot(hn, wg_ref[...])
        up = _dot(hn, wu_ref[...])
        act = (gate * jax.nn.sigmoid(gate) * up).astype(BF16)
        for c0 in range(0, n_out, FFN_DOWN_COLS):
            cols = slice(c0, c0 + FFN_DOWN_COLS)
            part = _dot(act, wd_ref[:, cols])
            if init:
                o_ref[sl, cols] = part
            else:
                o_ref[sl, cols] += part

    @pl.when(j == 0)
    def _():
        for q in range(n_split):
            sl = slice(q * rows, (q + 1) * rows)
            x = x_ref[sl, :]
            ms = jnp.mean(x * x, axis=-1, keepdims=True)
            y = x * lax.rsqrt(ms + RMS_EPS) * g2_ref[...]
            hn_ref[sl, :] = (y * (1.0 + sc_ref[0]) + sh_ref[0]).astype(BF16)
            swiglu_step(sl, True)

    @pl.when((j > 0) & (j < last))
    def _():
        swiglu_step(slice(0, tm), False)

    @pl.when(j == last)
    def _():
        for q in range(n_split):
            sl = slice(q * rows, (q + 1) * rows)
            swiglu_step(sl, False)
            x = x_ref[sl, :] + gt_ref[0] * o_ref[sl, :]
            ms = jnp.mean(x * x, axis=-1, keepdims=True)
            o_ref[sl, :] = x * lax.rsqrt(ms + RMS_EPS) * gf_ref[...]


def _ffn(x1, sc, sh, gt, g2, gf, wg, wu, wd, seq):
    m, d = x1.shape
    tm, tf, n_split = 1024, 512, 4
    per_seq = seq // tm
    row = lambda i, j: (i, 0)
    mod = lambda i, j: (i // per_seq, 0, 0)
    const = lambda i, j: (0, 0)
    return pl.pallas_call(
        functools.partial(_ffn_kernel, n_split),
        grid=(m // tm, D_FF // tf),
        in_specs=[pl.BlockSpec((tm, d), row),
                  pl.BlockSpec((1, 1, d), mod),
                  pl.BlockSpec((1, 1, d), mod),
                  pl.BlockSpec((1, 1, d), mod),
                  pl.BlockSpec((1, d), const),
                  pl.BlockSpec((1, d), const),
                  pl.BlockSpec((d, tf), lambda i, j: (0, j)),
                  pl.BlockSpec((d, tf), lambda i, j: (0, j)),
                  pl.BlockSpec((tf, d), lambda i, j: (j, 0))],
        out_specs=pl.BlockSpec((tm, d), row),
        out_shape=jax.ShapeDtypeStruct((m, d), F32),
        scratch_shapes=[pltpu.VMEM((tm, d), BF16)],
        compiler_params=_cparams(2),
        name="ffn",
    )(x1, sc, sh, gt, g2, gf, wg, wu, wd)


def _layer_weights(w_in, mu_shift, w2_decay, a2, g2_gate):
    d = w_in.shape[0]
    pad = H_COLS - (RKV_COLS + CONV_COLS + SMALL_COLS)
    w_pad = jnp.concatenate([w_in[:, :RKV_COLS], w_in[:, RKV_COLS + SMALL_COLS:],
                             w_in[:, RKV_COLS:RKV_COLS + SMALL_COLS], jnp.zeros((d, pad), F32)],
                            axis=1).astype(BF16)
    mu_pad = jnp.concatenate([mu_shift[:RKV_COLS], jnp.zeros((CONV_COLS,), F32), mu_shift[RKV_COLS:],
                              jnp.zeros((pad,), F32)]).reshape(1, H_COLS)
    zeros = lambda rows, cols: jnp.zeros((rows, cols), F32)
    wfb = jnp.concatenate([
        jnp.concatenate([w2_decay[0], zeros(DECAY_RANK, RWKV_WIDTH)], axis=1),
        jnp.concatenate([zeros(DECAY_RANK, RWKV_WIDTH), w2_decay[1]], axis=1)], axis=0).astype(BF16)
    wa = jnp.concatenate([a2, zeros(LANES - AAA_RANK, RWKV_WIDTH)], axis=0).astype(BF16)
    g_rows = SMALL_PAD - LANES
    wg = jnp.concatenate([zeros(AAA_RANK, RWKV_WIDTH), g2_gate,
                          zeros(g_rows - AAA_RANK - GATE_RANK, RWKV_WIDTH)], axis=0).astype(BF16)
    return w_pad, mu_pad, wfb, wa, wg


def _group(x, mod, lw):
    nb, seq, d = x.shape
    x2 = x.reshape(nb * seq, d)
    sh1, sc1, gt1, sh2, sc2, gt2 = [mod[:, q * d:(q + 1) * d].reshape(nb, 1, d) for q in range(6)]
    h = _in_proj(x2, sc1, sh1, lw["norm1_g"], lw["w_pad"], lw["mu_pad"], seq)
    k, kk, b, lwf, lwb, g, bonus = _prep(
        h, lw["wfb"], lw["wa"], lw["wg"], lw["w0"], lw["a0"], lw["k_k"], lw["k_a"], lw["r_k"])
    yf, yb = _scan(h, k, kk, b, lwf, lwb, seq)
    x1 = _post(yf, yb, g, bonus, h, lw["lnx_g"], lw["lnx_b"], lw["conv_w"], x2, gt1, lw["w_out"], seq)
    out = _ffn(x1, sc2, sh2, gt2, lw["norm2_g"], lw["norm_f_g"], lw["wg_ffn"], lw["wu_ffn"], lw["wd_ffn"], seq)
    return out.reshape(nb, seq, d)


def kernel(x_prompt, x_sample, c_prompt, c_sample, w_ada, b_ada, norm1_g, w_in, mu_shift, w0_decay, w2_decay, a0, a2, g2_gate, k_k, k_a, r_k, lnx_g, lnx_b, conv_w, w_out, norm2_g, w_ffn_gate, w_ffn_up, w_ffn_down, norm_f_g):
    assert w_ada.shape[0] == 1, "single-layer block"
    l = 0
    row = lambda t: t.reshape(1, -1)
    w_pad, mu_pad, wfb, wa, wg = _layer_weights(w_in[l], mu_shift[l], w2_decay[l], a2[l], g2_gate[l])
    lw = dict(w_pad=w_pad, mu_pad=mu_pad, wfb=wfb, wa=wa, wg=wg,
              norm1_g=row(norm1_g[l]), w0=w0_decay[l], a0=row(a0[l]), k_k=row(k_k[l]), k_a=row(k_a[l]),
              r_k=row(r_k[l]), lnx_g=row(lnx_g[l]), lnx_b=row(lnx_b[l]), conv_w=conv_w[l],
              w_out=w_out[l].astype(BF16), norm2_g=row(norm2_g[l]), norm_f_g=row(norm_f_g),
              wg_ffn=w_ffn_gate[l].astype(BF16), wu_ffn=w_ffn_up[l].astype(BF16),
              wd_ffn=w_ffn_down[l].astype(BF16))
    n_prompt = c_prompt.shape[0]
    mod = _mod(jnp.concatenate([c_prompt, c_sample], axis=0), w_ada[l], b_ada[l])
    y_prompt = _group(x_prompt, mod[:n_prompt], lw)
    y_sample = _group(x_sample, mod[n_prompt:], lw)
    return (y_prompt, y_sample)
```

```python
import functools
import math

import jax
import jax.numpy as jnp
from jax import lax
from jax.experimental import pallas as pl
from jax.experimental.pallas import tpu as pltpu

F32 = jnp.float32
BF16 = jnp.bfloat16

D_MODEL = 2048
RWKV_WIDTH = 1024
CONV_WIDTH = 1024
HEAD = 64
DECAY_RANK = 64
AAA_RANK = 64
GATE_RANK = 160
SMALL_COLS = 2 * DECAY_RANK + AAA_RANK + GATE_RANK
SMALL_PAD = 512
RKV_COLS = 3 * RWKV_WIDTH
CONV_COLS = 3 * CONV_WIDTH
RW_COLS = RKV_COLS + SMALL_PAD
D_FF = 5632
RMS_EPS = 1e-6
LNX_EPS = 64e-5

CHUNK = 64
LANES = 128
MXU_COLS = 256
HALO = 16
VMEM_LIMIT = 56 * 1024 * 1024
FFN_DOWN_COLS = 512
POST_OUT_COLS = 512


def _cparams(n_axes):
    return pltpu.CompilerParams(dimension_semantics=("arbitrary",) * n_axes,
                                vmem_limit_bytes=VMEM_LIMIT)


def _dot(a, b):
    return jnp.dot(a, b, preferred_element_type=F32)


def _dot_nt(a, b):
    return lax.dot_general(a, b, (((1,), (1,)), ((), ())), preferred_element_type=F32)


def _sigmoid(x):
    return 0.5 * jnp.tanh(0.5 * x) + 0.5


def _split_bf16(x):
    hi = x.astype(BF16)
    lo = (x - hi.astype(F32)).astype(BF16)
    return hi, lo


def _head_sum_bcast(x):
    n = MXU_COLS
    row = lax.broadcasted_iota(jnp.int32, (n, n), 0) >> 6
    col = lax.broadcasted_iota(jnp.int32, (n, n), 1) >> 6
    ones_bd = jnp.where(row == col, 1.0, 0.0).astype(BF16)
    outs = [_dot(x[:, q * n:(q + 1) * n].astype(BF16), ones_bd) for q in range(x.shape[1] // n)]
    return jnp.concatenate(outs, axis=1)


def _mod_kernel(c_ref, w_ref, b_ref, o_ref):
    c = c_ref[...]
    s = c * jax.nn.sigmoid(c)
    o_ref[...] = _dot(s.astype(BF16), w_ref[...].astype(BF16)) + b_ref[...]


def _mod(c, w_ada, b_ada):
    nb, d = c.shape
    n = w_ada.shape[1]
    tn = 1024
    return pl.pallas_call(
        _mod_kernel,
        grid=(n // tn,),
        in_specs=[pl.BlockSpec((nb, d), lambda j: (0, 0)),
                  pl.BlockSpec((d, tn), lambda j: (0, j)),
                  pl.BlockSpec((1, tn), lambda j: (0, j))],
        out_specs=pl.BlockSpec((nb, tn), lambda j: (0, j)),
        out_shape=jax.ShapeDtypeStruct((nb, n), F32),
        compiler_params=_cparams(1),
        name="mod",
    )(c, w_ada, b_ada.reshape(1, n))


def _token_shift_mix(h, prev_row, next_row, mu):
    rows, cols = h.shape
    r8 = lax.broadcasted_iota(jnp.int32, (8, cols), 0)
    even = (lax.broadcasted_iota(jnp.int32, (rows, cols), 1) & 1) == 0
    prv = pltpu.roll(h, 1, axis=0)
    prv = jnp.concatenate([jnp.where(r8 == 0, prev_row, prv[0:8]), prv[8:]], axis=0)
    nxt = pltpu.roll(h, rows - 1, axis=0)
    nxt = jnp.concatenate([nxt[:rows - 8], jnp.where(r8 == 7, next_row, nxt[rows - 8:])], axis=0)
    return h + mu * (jnp.where(even, prv, nxt) - h)


def _inproj_kernel(seq, n_split, x_ref, xp_ref, xn_ref, sc_ref, sh_ref, g_ref, wrw_ref, wcv_ref, mu_ref,
                   orw_ref, ocv_ref, hn_ref):
    tm = x_ref.shape[0]
    tn = orw_ref.shape[1]
    t0 = (pl.program_id(0) * tm) % seq
    j = pl.program_id(1)
    n_conv = pl.num_programs(1) - 1

    def normed(x):
        ms = jnp.mean(x * x, axis=-1, keepdims=True)
        y = x * lax.rsqrt(ms + RMS_EPS) * g_ref[...]
        return y * (1.0 + sc_ref[0]) + sh_ref[0]

    def mix_store(h_all, cols):
        prev_row = jnp.where(t0 == 0, 0.0, h_all[tm + 7:tm + 8])
        next_row = jnp.where(t0 + tm == seq, 0.0, h_all[tm + 8:tm + 9])
        orw_ref[:, cols] = _token_shift_mix(h_all[0:tm], prev_row, next_row, mu_ref[:, cols]).astype(orw_ref.dtype)

    def conv_tile():
        ocv_ref[...] = _dot(hn_ref[0:tm, :], wcv_ref[...]).astype(ocv_ref.dtype)

    def rwkv_tile():
        for c0 in range(0, tn, MXU_COLS):
            cols = slice(c0, c0 + MXU_COLS)
            mix_store(_dot(hn_ref[...], wrw_ref[:, cols]), cols)

    @pl.when(j == 0)
    def _():
        hn_ref[tm:, :] = jnp.concatenate([normed(xp_ref[...]), normed(xn_ref[...])], axis=0).astype(BF16)
        rows = tm // n_split
        parts = []
        for q in range(n_split):
            sl = slice(q * rows, (q + 1) * rows)
            hn_ref[sl, :] = normed(x_ref[sl, :]).astype(BF16)
            sl = slice(q * rows, (q + 1) * rows if q + 1 < n_split else tm + 16)
            parts.append(_dot(hn_ref[sl, :], wrw_ref[...]))
        h_all = jnp.concatenate(parts, axis=0)
        conv_tile()
        for c0 in range(0, tn, MXU_COLS):
            mix_store(h_all[:, c0:c0 + MXU_COLS], slice(c0, c0 + MXU_COLS))

    @pl.when((j > 0) & (j < n_conv))
    def _():
        rwkv_tile()
        conv_tile()

    @pl.when(j == n_conv)
    def _():
        rwkv_tile()


def _in_proj(x2, sc, sh, g1, w_rw, w_cv, mu_rw, seq):
    m, d = x2.shape
    tm, tn, n_split = 1024, 512, 4
    per_seq = seq // tm
    nblk8 = m // 8
    n_conv = CONV_COLS // tn
    assert RW_COLS // tn == n_conv + 1
    mod = lambda i, j: (i // per_seq, 0, 0)
    conv_col = lambda j: jnp.minimum(j, n_conv - 1)
    return pl.pallas_call(
        functools.partial(_inproj_kernel, seq, n_split),
        grid=(m // tm, n_conv + 1),
        in_specs=[pl.BlockSpec((tm, d), lambda i, j: (i, 0)),
                  pl.BlockSpec((8, d), lambda i, j: (jnp.maximum(i * (tm // 8) - 1, 0), 0)),
                  pl.BlockSpec((8, d), lambda i, j: (jnp.minimum((i + 1) * (tm // 8), nblk8 - 1), 0)),
                  pl.BlockSpec((1, 1, d), mod),
                  pl.BlockSpec((1, 1, d), mod),
                  pl.BlockSpec((1, d), lambda i, j: (0, 0)),
                  pl.BlockSpec((d, tn), lambda i, j: (0, j)),
                  pl.BlockSpec((d, tn), lambda i, j: (0, conv_col(j))),
                  pl.BlockSpec((1, tn), lambda i, j: (0, j))],
        out_specs=[pl.BlockSpec((tm, tn), lambda i, j: (i, j)),
                   pl.BlockSpec((tm, tn), lambda i, j: (i, conv_col(j)))],
        out_shape=[jax.ShapeDtypeStruct((m, RW_COLS), BF16), jax.ShapeDtypeStruct((m, CONV_COLS), BF16)],
        scratch_shapes=[pltpu.VMEM((tm + 16, d), BF16)],
        compiler_params=_cparams(2),
        name="in_proj",
    )(x2, x2, x2, sc, sh, g1, w_rw, w_cv, mu_rw)


def _prep_kernel(rkv_ref, sm_ref, wfb_ref, wa_ref, wg_ref,
                 w0_ref, a0_ref, kk_ref, ka_ref, rk_ref,
                 k_out, kk_out, b_out, lwf_out, lwb_out, g_out, bonus_out):
    xs_sm = sm_ref[...].astype(F32)
    lora_fb = _dot(jnp.tanh(xs_sm[:, 0:LANES]).astype(BF16), wfb_ref[...])
    lora_a = _dot(xs_sm[:, LANES:2 * LANES].astype(BF16), wa_ref[...])
    g = _dot(_sigmoid(xs_sm[:, LANES:]).astype(BF16), wg_ref[...])

    half_scale = 0.5 * math.exp(-0.5)
    w0 = w0_ref[...]
    lwf_out[...] = -half_scale * (1.0 + jnp.tanh(0.5 * (w0[0:1] + lora_fb[:, :RWKV_WIDTH])))
    lwb_out[...] = -half_scale * (1.0 + jnp.tanh(0.5 * (w0[1:2] + lora_fb[:, RWKV_WIDTH:])))
    g_out[...] = g.astype(g_out.dtype)
    a = _sigmoid(a0_ref[...] + lora_a)

    xk = rkv_ref[:, RWKV_WIDTH:2 * RWKV_WIDTH].astype(F32)
    kk_raw = xk * kk_ref[...]
    ss = _head_sum_bcast(kk_raw * kk_raw)
    kk = kk_raw / jnp.maximum(jnp.sqrt(ss), 1e-12)
    k = xk * (1.0 + (a - 1.0) * ka_ref[...])
    kk_out[...] = kk.astype(kk_out.dtype)
    b_out[...] = (kk * a).astype(b_out.dtype)
    k_out[...] = k.astype(k_out.dtype)
    r = rkv_ref[:, 0:RWKV_WIDTH].astype(F32)
    rk_sum = _head_sum_bcast(r * k * rk_ref[...])
    v = rkv_ref[:, 2 * RWKV_WIDTH:].astype(F32)
    bonus_out[...] = (rk_sum * v).astype(bonus_out.dtype)


def _prep(h, wfb, wa, wg, w0, a0, k_k, k_a, r_k):
    m = h.shape[0]
    tp = 512
    small_blk = RKV_COLS // SMALL_PAD
    row = lambda i: (i, 0)
    const = lambda i: (0, 0)
    out_dtypes = [BF16] * 3 + [F32, F32, BF16, BF16]
    return pl.pallas_call(
        _prep_kernel,
        grid=(m // tp,),
        in_specs=[pl.BlockSpec((tp, RKV_COLS), row),
                  pl.BlockSpec((tp, SMALL_PAD), lambda i: (i, small_blk)),
                  pl.BlockSpec(wfb.shape, const),
                  pl.BlockSpec(wa.shape, const),
                  pl.BlockSpec(wg.shape, const),
                  pl.BlockSpec((2, RWKV_WIDTH), const),
                  pl.BlockSpec((1, RWKV_WIDTH), const),
                  pl.BlockSpec((1, RWKV_WIDTH), const),
                  pl.BlockSpec((1, RWKV_WIDTH), const),
                  pl.BlockSpec((1, RWKV_WIDTH), const)],
        out_specs=[pl.BlockSpec((tp, RWKV_WIDTH), row)] * len(out_dtypes),
        out_shape=[jax.ShapeDtypeStruct((m, RWKV_WIDTH), dt) for dt in out_dtypes],
        compiler_params=_cparams(1),
        name="prep",
    )(h, h, wfb, wa, wg, w0, a0, k_k, k_a, r_k)


def _block_diag(x, lane_lo):
    return jnp.concatenate([jnp.where(lane_lo, x, 0.0).astype(BF16),
                            jnp.where(lane_lo, 0.0, x).astype(BF16)], axis=0)


def _chunk_parallel(chains, common, out):
    eye2, lane_lo, _ = common
    bd = lambda x: _block_diag(x, lane_lo)
    n = len(chains)
    at, rt, kt, bt, g_all = [], [], [], [], []
    for (r, k, v, kk, b, lw, cum, _, bwd) in chains:
        gq = jnp.exp(cum)
        gk = jnp.exp(-cum)
        g_all.append(gq[0:1] if bwd else gq[CHUNK - 1:CHUNK])
        at.append(-(kk * jnp.exp(cum - lw)))
        rt.append(r * gq)
        kt.append(k * gk)
        bt.append(b * gk)
    yield

    a_ak, a_ab, a_rk, a_rb = [], [], [], []
    for c in range(n):
        tri_strict, tri_incl = chains[c][7]
        lhs = jnp.concatenate([at[c], rt[c]], axis=0).astype(BF16)
        rhs = jnp.concatenate([bd(kt[c]), bd(bt[c])], axis=0)
        big = _dot_nt(lhs, rhs)
        a_ak.append(jnp.where(tri_strict, big[0:CHUNK, 0:LANES], 0.0))
        a_ab.append(jnp.where(tri_strict, big[0:CHUNK, LANES:], 0.0))
        a_rk.append(jnp.where(tri_incl, big[CHUNK:, 0:LANES], 0.0))
        a_rb.append(jnp.where(tri_incl, big[CHUNK:, LANES:], 0.0))
    yield

    q = [eye2 + a for a in a_ab]
    p = [_dot(a.astype(BF16), bd(a)) for a in a_ab]
    xy = [_dot(jnp.concatenate([a_ak[c], a_rk[c]], axis=0).astype(BF16), bd(chains[c][2])) for c in range(n)]
    yield
    for _ in range(4):
        pq_pp = [_dot(p[c].astype(BF16), jnp.concatenate([bd(q[c]), bd(p[c])], axis=1)) for c in range(n)]
        q = [q[c] + pq_pp[c][:, 0:LANES] for c in range(n)]
        p = [pq_pp[c][:, LANES:] for c in range(n)]
        yield
    q = [q[c] + _dot(p[c].astype(BF16), bd(q[c])) for c in range(n)]
    yield
    wu = [_dot(q[c].astype(BF16), jnp.concatenate([bd(at[c]), bd(xy[c][0:CHUNK])], axis=1)) for c in range(n)]
    out.update(rt=rt, kt=kt, bt=bt, g_all=g_all, a_rb=a_rb, xy=xy, wu=wu, v=[ch[2] for ch in chains])


def _chunk_sequential(par, states, ys, common):
    _, lane_lo, bd_mask = common
    bd = lambda x: _block_diag(x, lane_lo)
    n = len(states)
    wu, rt, xy = par["wu"], par["rt"], par["xy"]
    s1 = [_dot_nt(jnp.concatenate([wu[c][:, 0:LANES], rt[c]], axis=0).astype(BF16), states[c].astype(BF16))
          for c in range(n)]
    u = [wu[c][:, LANES:] + s1[c][0:CHUNK] for c in range(n)]
    yield
    ys.extend(s1[c][CHUNK:] + xy[c][CHUNK:] + _dot(par["a_rb"][c].astype(BF16), bd(u[c])) for c in range(n))
    yield
    for c in range(n):
        vu_t = jnp.concatenate([par["v"][c], u[c]], axis=0).T.astype(BF16)
        kb = jnp.concatenate([par["kt"][c], par["bt"][c]], axis=0).astype(BF16)
        states[c] = jnp.where(bd_mask, (states[c] + _dot(vu_t, kb)) * par["g_all"][c], 0.0)
    yield


def _interleave(*gens):
    gens = list(gens)
    while gens:
        for g in list(gens):
            try:
                next(g)
            except StopIteration:
                gens.remove(g)


def _chain_gens(*gens):
    for g in gens:
        yield from g


def _scan_kernel(n_chunks,
                 rf_ref, kf_ref, vf_ref, kkf_ref, bf_ref, lwf_ref,
                 rb_ref, kb_ref, vb_ref, kkb_ref, bb_ref, lwb_ref,
                 yf_ref, yb_ref, state_ref, cumf_ref, cumb_ref):
    @pl.when(pl.program_id(1) == 0)
    def _():
        state_ref[...] = jnp.zeros_like(state_ref)

    n_pairs = rf_ref.shape[1] // LANES
    tb = n_chunks * CHUNK
    ri = lax.broadcasted_iota(jnp.int32, (CHUNK, LANES), 0)
    ci = lax.broadcasted_iota(jnp.int32, (CHUNK, LANES), 1) & (CHUNK - 1)
    lane_lo = lax.broadcasted_iota(jnp.int32, (CHUNK, LANES), 1) < CHUNK
    eye2 = jnp.where(ri == ci, 1.0, 0.0)
    br = lax.broadcasted_iota(jnp.int32, (LANES, LANES), 0) >> 6
    bc = lax.broadcasted_iota(jnp.int32, (LANES, LANES), 1) >> 6
    common = (eye2, lane_lo, br == bc)
    tri_f = (ci < ri, ci <= ri)
    tri_b = (ci > ri, ci >= ri)

    tr = lax.broadcasted_iota(jnp.int32, (tb, tb), 0)
    tc = lax.broadcasted_iota(jnp.int32, (tb, tb), 1)
    same = (tr >> 6) == (tc >> 6)
    ones_f = jnp.where(same & (tc <= tr), 1.0, 0.0).astype(BF16)
    ones_b = jnp.where(same & (tc >= tr), 1.0, 0.0).astype(BF16)

    def cumsum(ones, lw):
        hi, lo = _split_bf16(lw)
        return _dot(ones, hi) + _dot(ones, lo)

    cumf_ref[...] = cumsum(ones_f, lwf_ref[...])
    cumb_ref[...] = cumsum(ones_b, lwb_ref[...])

    def rows(step):
        return (slice(step * CHUNK, (step + 1) * CHUNK),
                slice((n_chunks - 1 - step) * CHUNK, (n_chunks - step) * CHUNK))

    def parallel(step, out):
        rows_f, rows_b = rows(step)
        chains = []
        for p in range(n_pairs):
            cols = slice(p * LANES, (p + 1) * LANES)
            tok_f = [ref[rows_f, cols].astype(F32) for ref in (rf_ref, kf_ref, vf_ref, kkf_ref, bf_ref)]
            tok_b = [ref[rows_b, cols].astype(F32) for ref in (rb_ref, kb_ref, vb_ref, kkb_ref, bb_ref)]
            chains.append((*tok_f, lwf_ref[rows_f, cols], cumf_ref[rows_f, cols], tri_f, False))
            chains.append((*tok_b, lwb_ref[rows_b, cols], cumb_ref[rows_b, cols], tri_b, True))
        yield from _chunk_parallel(chains, common, out)

    def sequential(step, par):
        rows_f, rows_b = rows(step)
        ys = []
        yield from _chunk_sequential(par, states, ys, common)
        for p in range(n_pairs):
            cols = slice(p * LANES, (p + 1) * LANES)
            yf_ref[rows_f, cols] = ys[2 * p].astype(yf_ref.dtype)
            yb_ref[rows_b, cols] = ys[2 * p + 1].astype(yb_ref.dtype)

    states = [state_ref[d, p] for p in range(n_pairs) for d in range(2)]
    pending = None
    for step in range(0, n_chunks, 2):
        par = [{}, {}]
        gens = [parallel(step, par[0]), parallel(step + 1, par[1])]
        if pending is not None:
            gens.append(_chain_gens(sequential(step - 2, pending[0]), sequential(step - 1, pending[1])))
        _interleave(*gens)
        pending = par
    _interleave(_chain_gens(sequential(n_chunks - 2, pending[0]), sequential(n_chunks - 1, pending[1])))
    for p in range(n_pairs):
        for d in range(2):
            state_ref[d, p] = states[2 * p + d]


def _scan(h, k, kk, b, lwf, lwb, seq):
    m = k.shape[0]
    n_chunks = 4
    tb = n_chunks * CHUNK
    nb = seq // tb
    fwd = lambda s, i: (s * nb + i, 0)
    bwd = lambda s, i: (s * nb + nb - 1 - i, 0)
    blk = lambda im, col=0: pl.BlockSpec((tb, RWKV_WIDTH), lambda s, i: (im(s, i)[0], col))
    out = jax.ShapeDtypeStruct((m, RWKV_WIDTH), BF16)
    specs = lambda im: [blk(im, 0), blk(im), blk(im, 2), blk(im), blk(im), blk(im)]
    return pl.pallas_call(
        functools.partial(_scan_kernel, n_chunks),
        grid=(m // seq, nb),
        in_specs=specs(fwd) + specs(bwd),
        out_specs=[blk(fwd), blk(bwd)],
        out_shape=[out, out],
        scratch_shapes=[pltpu.VMEM((2, RWKV_WIDTH // LANES, LANES, LANES), F32),
                        pltpu.VMEM((tb, RWKV_WIDTH), F32),
                        pltpu.VMEM((tb, RWKV_WIDTH), F32)],
        compiler_params=_cparams(2),
        name="scan",
    )(h, k, h, kk, b, lwf, h, k, h, kk, b, lwb)


def _post_kernel(seq,
                 yf_ref, yb_ref, g_ref, bonus_ref, cv_ref, cvp_ref, cvn_ref,
                 lng_ref, lnb_ref, cw_ref, x_ref, gt_ref, w_ref, o_ref, mix_ref):
    tm = yf_ref.shape[0]
    t0 = (pl.program_id(0) * tm) % seq
    first = t0 == 0
    last = t0 + tm == seq
    y = yf_ref[...].astype(F32) + yb_ref[...].astype(F32)
    mean = _head_sum_bcast(y) * (1.0 / HEAD)
    d = y - mean
    var = _head_sum_bcast(d * d) * (1.0 / HEAD)
    yn = d * lax.rsqrt(var + LNX_EPS) * lng_ref[...] + lnb_ref[...]
    mix_ref[:, 0:RWKV_WIDTH] = ((yn + bonus_ref[...].astype(F32)) * g_ref[...].astype(F32)).astype(BF16)

    c1, c2 = slice(CONV_WIDTH, 2 * CONV_WIDTH), slice(2 * CONV_WIDTH, 3 * CONV_WIDTH)
    f32 = lambda t: t.astype(F32)
    z = f32(cv_ref[:, c1]) * f32(cv_ref[:, c2])
    hp = f32(cvp_ref[...])[HALO - 1:HALO]
    hn = f32(cvn_ref[...])[0:1]
    zp = jnp.where(first, 0.0, hp[:, c1] * hp[:, c2])
    zn = jnp.where(last, 0.0, hn[:, c1] * hn[:, c2])
    ridx = lax.broadcasted_iota(jnp.int32, z.shape, 0)
    z_prev = jnp.where(ridx == 0, zp, pltpu.roll(z, 1, axis=0))
    z_next = jnp.where(ridx == tm - 1, zn, pltpu.roll(z, tm - 1, axis=0))
    cw = cw_ref[...]
    conv = z_prev * cw[0:1] + z * cw[1:2] + z_next * cw[2:3]
    mix_ref[:, RWKV_WIDTH:] = (f32(cv_ref[:, 0:CONV_WIDTH]) * conv).astype(BF16)

    gt = gt_ref[0]
    for c0 in range(0, o_ref.shape[1], POST_OUT_COLS):
        cols = slice(c0, c0 + POST_OUT_COLS)
        o_ref[:, cols] = x_ref[:, cols] + gt[:, cols] * _dot(mix_ref[...], w_ref[:, cols])


def _post(yf, yb, g, bonus, h, lnx_g, lnx_b, conv_w, x2, gt1, w_out, seq):
    m, d = x2.shape
    tm = 512
    per_seq = seq // tm
    n_halo = m // HALO
    row = lambda i: (i, 0)
    const = lambda i: (0, 0)
    return pl.pallas_call(
        functools.partial(_post_kernel, seq),
        grid=(m // tm,),
        in_specs=[pl.BlockSpec((tm, RWKV_WIDTH), row)] * 4 + [
            pl.BlockSpec((tm, CONV_COLS), row),
            pl.BlockSpec((HALO, CONV_COLS), lambda i: (jnp.maximum(i * (tm // HALO) - 1, 0), 0)),
            pl.BlockSpec((HALO, CONV_COLS), lambda i: (jnp.minimum((i + 1) * (tm // HALO), n_halo - 1), 0)),
            pl.BlockSpec((1, RWKV_WIDTH), const),
            pl.BlockSpec((1, RWKV_WIDTH), const),
            pl.BlockSpec((3, CONV_WIDTH), const),
            pl.BlockSpec((tm, d), row),
            pl.BlockSpec((1, 1, d), lambda i: (i // per_seq, 0, 0)),
            pl.BlockSpec((d, d), const, pipeline_mode=pl.Buffered(1))],
        out_specs=pl.BlockSpec((tm, d), row),
        out_shape=jax.ShapeDtypeStruct((m, d), F32),
        scratch_shapes=[pltpu.VMEM((tm, d), BF16)],
        compiler_params=_cparams(1),
        name="post",
    )(yf, yb, g, bonus, h, h, h, lnx_g, lnx_b, conv_w, x2, gt1, w_out)


def _ffn_kernel(n_split, x_ref, sc_ref, sh_ref, gt_ref, g2_ref, gf_ref, wg_ref, wu_ref, wd_ref, o_ref, hn_ref):
    j = pl.program_id(1)
    last = pl.num_programs(1) - 1
    tm, n_out = o_ref.shape
    rows = tm // n_split

    def swiglu_step(sl, init):
        hn = hn_ref[sl, :]
        gate = _dot(hn, wg_ref[...])
        up = _dot(hn, wu_ref[...])
        act = (gate * jax.nn.sigmoid(gate) * up).astype(BF16)
        for c0 in range(0, n_out, FFN_DOWN_COLS):
            cols = slice(c0, c0 + FFN_DOWN_COLS)
            part = _dot(act, wd_ref[:, cols])
            if init:
                o_ref[sl, cols] = part
            else:
                o_ref[sl, cols] += part

    @pl.when(j == 0)
    def _():
        for q in range(n_split):
            sl = slice(q * rows, (q + 1) * rows)
            x = x_ref[sl, :]
            ms = jnp.mean(x * x, axis=-1, keepdims=True)
            y = x * lax.rsqrt(ms + RMS_EPS) * g2_ref[...]
            hn_ref[sl, :] = (y * (1.0 + sc_ref[0]) + sh_ref[0]).astype(BF16)
            swiglu_step(sl, True)

    @pl.when((j > 0) & (j < last))
    def _():
        swiglu_step(slice(0, tm), False)

    @pl.when(j == last)
    def _():
        for q in range(n_split):
            sl = slice(q * rows, (q + 1) * rows)
            swiglu_step(sl, False)
            x = x_ref[sl, :] + gt_ref[0] * o_ref[sl, :]
            ms = jnp.mean(x * x, axis=-1, keepdims=True)
            o_ref[sl, :] = x * lax.rsqrt(ms + RMS_EPS) * gf_ref[...]


def _ffn(x1, sc, sh, gt, g2, gf, wg, wu, wd, seq):
    m, d = x1.shape
    tm, tf, n_split = 1024, 512, 4
    per_seq = seq // tm
    row = lambda i, j: (i, 0)
    mod = lambda i, j: (i // per_seq, 0, 0)
    const = lambda i, j: (0, 0)
    return pl.pallas_call(
        functools.partial(_ffn_kernel, n_split),
        grid=(m // tm, D_FF // tf),
        in_specs=[pl.BlockSpec((tm, d), row),
                  pl.BlockSpec((1, 1, d), mod),
                  pl.BlockSpec((1, 1, d), mod),
                  pl.BlockSpec((1, 1, d), mod),
                  pl.BlockSpec((1, d), const),
                  pl.BlockSpec((1, d), const),
                  pl.BlockSpec((d, tf), lambda i, j: (0, j)),
                  pl.BlockSpec((d, tf), lambda i, j: (0, j)),
                  pl.BlockSpec((tf, d), lambda i, j: (j, 0))],
        out_specs=pl.BlockSpec((tm, d), row),
        out_shape=jax.ShapeDtypeStruct((m, d), F32),
        scratch_shapes=[pltpu.VMEM((tm, d), BF16)],
        compiler_params=_cparams(2),
        name="ffn",
    )(x1, sc, sh, gt, g2, gf, wg, wu, wd)


def _layer_weights(w_in, mu_shift, w2_decay, a2, g2_gate):
    n_rw = RKV_COLS + SMALL_COLS
    pad = RW_COLS - n_rw
    w_rw = jnp.pad(w_in[:, :n_rw].astype(BF16), ((0, 0), (0, pad)))
    w_cv = w_in[:, n_rw:].astype(BF16)
    mu_rw = jnp.pad(mu_shift, (0, pad)).reshape(1, RW_COLS)
    zeros = lambda rows, cols: jnp.zeros((rows, cols), F32)
    wfb = jnp.concatenate([
        jnp.concatenate([w2_decay[0], zeros(DECAY_RANK, RWKV_WIDTH)], axis=1),
        jnp.concatenate([zeros(DECAY_RANK, RWKV_WIDTH), w2_decay[1]], axis=1)], axis=0).astype(BF16)
    wa = jnp.concatenate([a2, zeros(LANES - AAA_RANK, RWKV_WIDTH)], axis=0).astype(BF16)
    g_rows = SMALL_PAD - LANES
    wg = jnp.concatenate([zeros(AAA_RANK, RWKV_WIDTH), g2_gate,
                          zeros(g_rows - AAA_RANK - GATE_RANK, RWKV_WIDTH)], axis=0).astype(BF16)
    return w_rw, w_cv, mu_rw, wfb, wa, wg


def _group(x, mod, lw):
    nb, seq, d = x.shape
    x2 = x.reshape(nb * seq, d)
    sh1, sc1, gt1, sh2, sc2, gt2 = [mod[:, q * d:(q + 1) * d].reshape(nb, 1, d) for q in range(6)]
    h, h_cv = _in_proj(x2, sc1, sh1, lw["norm1_g"], lw["w_rw"], lw["w_cv"], lw["mu_rw"], seq)
    k, kk, b, lwf, lwb, g, bonus = _prep(
        h, lw["wfb"], lw["wa"], lw["wg"], lw["w0"], lw["a0"], lw["k_k"], lw["k_a"], lw["r_k"])
    yf, yb = _scan(h, k, kk, b, lwf, lwb, seq)
    x1 = _post(yf, yb, g, bonus, h_cv, lw["lnx_g"], lw["lnx_b"], lw["conv_w"], x2, gt1, lw["w_out"], seq)
    out = _ffn(x1, sc2, sh2, gt2, lw["norm2_g"], lw["norm_f_g"], lw["wg_ffn"], lw["wu_ffn"], lw["wd_ffn"], seq)
    return out.reshape(nb, seq, d)


def kernel(x_prompt, x_sample, c_prompt, c_sample, w_ada, b_ada, norm1_g, w_in, mu_shift, w0_decay, w2_decay, a0, a2, g2_gate, k_k, k_a, r_k, lnx_g, lnx_b, conv_w, w_out, norm2_g, w_ffn_gate, w_ffn_up, w_ffn_down, norm_f_g):
    assert w_ada.shape[0] == 1, "single-layer block"
    l = 0
    row = lambda t: t.reshape(1, -1)
    w_rw, w_cv, mu_rw, wfb, wa, wg = _layer_weights(w_in[l], mu_shift[l], w2_decay[l], a2[l], g2_gate[l])
    lw = dict(w_rw=w_rw, w_cv=w_cv, mu_rw=mu_rw, wfb=wfb, wa=wa, wg=wg,
              norm1_g=row(norm1_g[l]), w0=w0_decay[l], a0=row(a0[l]), k_k=row(k_k[l]), k_a=row(k_a[l]),
              r_k=row(r_k[l]), lnx_g=row(lnx_g[l]), lnx_b=row(lnx_b[l]), conv_w=conv_w[l],
              w_out=w_out[l].astype(BF16), norm2_g=row(norm2_g[l]), norm_f_g=row(norm_f_g),
              wg_ffn=w_ffn_gate[l].astype(BF16), wu_ffn=w_ffn_up[l].astype(BF16),
              wd_ffn=w_ffn_down[l].astype(BF16))
    n_prompt = c_prompt.shape[0]
    mod = _mod(jnp.concatenate([c_prompt, c_sample], axis=0), w_ada[l], b_ada[l])
    y_prompt = _group(x_prompt, mod[:n_prompt], lw)
    y_sample = _group(x_sample, mod[n_prompt:], lw)
    return (y_prompt, y_sample)
```

```python
import functools
import math

import jax
import jax.numpy as jnp
from jax import lax
from jax.experimental import pallas as pl
from jax.experimental.pallas import tpu as pltpu

F32 = jnp.float32
BF16 = jnp.bfloat16

D_MODEL = 2048
RWKV_WIDTH = 1024
CONV_WIDTH = 1024
HEAD = 64
DECAY_RANK = 64
AAA_RANK = 64
GATE_RANK = 160
SMALL_COLS = 2 * DECAY_RANK + AAA_RANK + GATE_RANK
SMALL_PAD = 512
RKV_COLS = 3 * RWKV_WIDTH
CONV_COLS = 3 * CONV_WIDTH
RW_COLS = RKV_COLS + SMALL_PAD
D_FF = 5632
RMS_EPS = 1e-6
LNX_EPS = 64e-5

CHUNK = 64
LANES = 128
MXU_COLS = 256
HALO = 16
COL_TILE = 512
FFN_TILE = 512
VMEM_LIMIT = 56 * 1024 * 1024
FFN_DOWN_COLS = 512
POST_OUT_COLS = 512


def _cparams(n_axes):
    return pltpu.CompilerParams(dimension_semantics=("arbitrary",) * n_axes,
                                vmem_limit_bytes=VMEM_LIMIT)


def _dot(a, b):
    return jnp.dot(a, b, preferred_element_type=F32)


def _dot_nt(a, b):
    return lax.dot_general(a, b, (((1,), (1,)), ((), ())), preferred_element_type=F32)


def _tile_cols(ref, c0, width, rows=slice(None)):
    tiles = range(c0 // COL_TILE, (c0 + width) // COL_TILE)
    return jnp.concatenate([ref[t, rows, :] for t in tiles], axis=1)


def _sigmoid(x):
    return 0.5 * jnp.tanh(0.5 * x) + 0.5


def _split_bf16(x):
    hi = x.astype(BF16)
    lo = (x - hi.astype(F32)).astype(BF16)
    return hi, lo


def _head_sum_bcast(x):
    n = MXU_COLS
    row = lax.broadcasted_iota(jnp.int32, (n, n), 0) >> 6
    col = lax.broadcasted_iota(jnp.int32, (n, n), 1) >> 6
    ones_bd = jnp.where(row == col, 1.0, 0.0).astype(BF16)
    outs = [_dot(x[:, q * n:(q + 1) * n].astype(BF16), ones_bd) for q in range(x.shape[1] // n)]
    return jnp.concatenate(outs, axis=1)


def _mod_kernel(c_ref, w_ref, b_ref, o_ref):
    c = c_ref[...]
    s = c * jax.nn.sigmoid(c)
    o_ref[...] = _dot(s.astype(BF16), w_ref[...].astype(BF16)) + b_ref[...]


def _mod(c, w_ada, b_ada):
    nb, d = c.shape
    n = w_ada.shape[1]
    tn = 1024
    return pl.pallas_call(
        _mod_kernel,
        grid=(n // tn,),
        in_specs=[pl.BlockSpec((nb, d), lambda j: (0, 0)),
                  pl.BlockSpec((d, tn), lambda j: (0, j)),
                  pl.BlockSpec((1, tn), lambda j: (0, j))],
        out_specs=pl.BlockSpec((nb, tn), lambda j: (0, j)),
        out_shape=jax.ShapeDtypeStruct((nb, n), F32),
        compiler_params=_cparams(1),
        name="mod",
    )(c, w_ada, b_ada.reshape(1, n))


def _token_shift_mix(h, prev_row, next_row, mu):
    rows, cols = h.shape
    r8 = lax.broadcasted_iota(jnp.int32, (8, cols), 0)
    even = (lax.broadcasted_iota(jnp.int32, (rows, cols), 1) & 1) == 0
    prv = pltpu.roll(h, 1, axis=0)
    prv = jnp.concatenate([jnp.where(r8 == 0, prev_row, prv[0:8]), prv[8:]], axis=0)
    nxt = pltpu.roll(h, rows - 1, axis=0)
    nxt = jnp.concatenate([nxt[:rows - 8], jnp.where(r8 == 7, next_row, nxt[rows - 8:])], axis=0)
    return h + mu * (jnp.where(even, prv, nxt) - h)


def _inproj_kernel(seq, n_split, x_ref, xp_ref, xn_ref, sc_ref, sh_ref, g_ref, wrw_ref, wcv_ref, mu_ref,
                   orw_ref, ocv_ref, hn_ref):
    tm = x_ref.shape[0]
    tn = orw_ref.shape[1]
    t0 = (pl.program_id(0) * tm) % seq
    j = pl.program_id(1)

    def normed(x):
        ms = jnp.mean(x * x, axis=-1, keepdims=True)
        y = x * lax.rsqrt(ms + RMS_EPS) * g_ref[...]
        return y * (1.0 + sc_ref[0]) + sh_ref[0]

    def mix_store(h_all, cols):
        prev_row = jnp.where(t0 == 0, 0.0, h_all[tm + 7:tm + 8])
        next_row = jnp.where(t0 + tm == seq, 0.0, h_all[tm + 8:tm + 9])
        orw_ref[:, cols] = _token_shift_mix(h_all[0:tm], prev_row, next_row, mu_ref[:, cols]).astype(orw_ref.dtype)

    def conv_tile():
        ocv_ref[...] = _dot(hn_ref[0:tm, :], wcv_ref[...]).astype(ocv_ref.dtype)

    def rwkv_tile():
        for c0 in range(0, tn, MXU_COLS):
            cols = slice(c0, c0 + MXU_COLS)
            mix_store(_dot(hn_ref[...], wrw_ref[:, cols]), cols)

    @pl.when(j == 0)
    def _():
        hn_ref[tm:, :] = jnp.concatenate([normed(xp_ref[...]), normed(xn_ref[...])], axis=0).astype(BF16)
        rows = tm // n_split
        parts = []
        for q in range(n_split):
            sl = slice(q * rows, (q + 1) * rows)
            hn_ref[sl, :] = normed(x_ref[sl, :]).astype(BF16)
            sl = slice(q * rows, (q + 1) * rows if q + 1 < n_split else tm + 16)
            parts.append(_dot(hn_ref[sl, :], wrw_ref[...]))
        h_all = jnp.concatenate(parts, axis=0)
        for c0 in range(0, tn, MXU_COLS):
            mix_store(h_all[:, c0:c0 + MXU_COLS], slice(c0, c0 + MXU_COLS))

    @pl.when(j > 0)
    def _():
        rwkv_tile()
        conv_tile()


def _in_proj(x2, sc, sh, g1, w_rw, w_cv, mu_rw, seq):
    m, d = x2.shape
    tm, tn, n_split = 1024, COL_TILE, 4
    per_seq = seq // tm
    nblk8 = m // 8
    n_rw, n_conv = w_rw.shape[0], w_cv.shape[0]
    assert n_rw == n_conv + 1
    mod = lambda i, j: (i // per_seq, 0, 0)
    rw_tile = lambda j: (j + n_conv) % n_rw
    conv_tile = lambda j: jnp.maximum(j - 1, 0)
    return pl.pallas_call(
        functools.partial(_inproj_kernel, seq, n_split),
        grid=(m // tm, n_rw),
        in_specs=[pl.BlockSpec((tm, d), lambda i, j: (i, 0)),
                  pl.BlockSpec((8, d), lambda i, j: (jnp.maximum(i * (tm // 8) - 1, 0), 0)),
                  pl.BlockSpec((8, d), lambda i, j: (jnp.minimum((i + 1) * (tm // 8), nblk8 - 1), 0)),
                  pl.BlockSpec((1, 1, d), mod),
                  pl.BlockSpec((1, 1, d), mod),
                  pl.BlockSpec((1, d), lambda i, j: (0, 0)),
                  pl.BlockSpec((None, d, tn), lambda i, j: (rw_tile(j), 0, 0)),
                  pl.BlockSpec((None, d, tn), lambda i, j: (conv_tile(j), 0, 0)),
                  pl.BlockSpec((1, tn), lambda i, j: (0, rw_tile(j)))],
        out_specs=[pl.BlockSpec((None, tm, tn), lambda i, j: (rw_tile(j), i, 0)),
                   pl.BlockSpec((None, tm, tn), lambda i, j: (conv_tile(j), i, 0))],
        out_shape=[jax.ShapeDtypeStruct((n_rw, m, tn), BF16), jax.ShapeDtypeStruct((n_conv, m, tn), BF16)],
        scratch_shapes=[pltpu.VMEM((tm + 16, d), BF16)],
        compiler_params=_cparams(2),
        name="in_proj",
    )(x2, x2, x2, sc, sh, g1, w_rw, w_cv, mu_rw)


def _prep_kernel(h_ref, wfb_ref, wa_ref, wg_ref,
                 w0_ref, a0_ref, kk_ref, ka_ref, rk_ref,
                 k_out, kk_out, b_out, lwf_out, lwb_out, g_out, bonus_out):
    xs_sm = h_ref[RKV_COLS // COL_TILE].astype(F32)
    lora_fb = _dot(jnp.tanh(xs_sm[:, 0:LANES]).astype(BF16), wfb_ref[...])
    lora_a = _dot(xs_sm[:, LANES:2 * LANES].astype(BF16), wa_ref[...])
    g = _dot(_sigmoid(xs_sm[:, LANES:]).astype(BF16), wg_ref[...])

    half_scale = 0.5 * math.exp(-0.5)
    w0 = w0_ref[...]
    lwf_out[...] = -half_scale * (1.0 + jnp.tanh(0.5 * (w0[0:1] + lora_fb[:, :RWKV_WIDTH])))
    lwb_out[...] = -half_scale * (1.0 + jnp.tanh(0.5 * (w0[1:2] + lora_fb[:, RWKV_WIDTH:])))
    g_out[...] = g.astype(g_out.dtype)
    a = _sigmoid(a0_ref[...] + lora_a)

    xk = _tile_cols(h_ref, RWKV_WIDTH, RWKV_WIDTH).astype(F32)
    kk_raw = xk * kk_ref[...]
    ss = _head_sum_bcast(kk_raw * kk_raw)
    kk = kk_raw / jnp.maximum(jnp.sqrt(ss), 1e-12)
    k = xk * (1.0 + (a - 1.0) * ka_ref[...])
    kk_out[...] = kk.astype(kk_out.dtype)
    b_out[...] = (kk * a).astype(b_out.dtype)
    k_out[...] = k.astype(k_out.dtype)
    r = _tile_cols(h_ref, 0, RWKV_WIDTH).astype(F32)
    rk_sum = _head_sum_bcast(r * k * rk_ref[...])
    v = _tile_cols(h_ref, 2 * RWKV_WIDTH, RWKV_WIDTH).astype(F32)
    bonus_out[...] = (rk_sum * v).astype(bonus_out.dtype)


def _prep(h, wfb, wa, wg, w0, a0, k_k, k_a, r_k):
    n_tiles, m, _ = h.shape
    tp = 512
    row = lambda i: (i, 0)
    const = lambda i: (0, 0)
    out_dtypes = [BF16] * 3 + [F32, F32, BF16, BF16]
    return pl.pallas_call(
        _prep_kernel,
        grid=(m // tp,),
        in_specs=[pl.BlockSpec((n_tiles, tp, COL_TILE), lambda i: (0, i, 0)),
                  pl.BlockSpec(wfb.shape, const),
                  pl.BlockSpec(wa.shape, const),
                  pl.BlockSpec(wg.shape, const),
                  pl.BlockSpec((2, RWKV_WIDTH), const),
                  pl.BlockSpec((1, RWKV_WIDTH), const),
                  pl.BlockSpec((1, RWKV_WIDTH), const),
                  pl.BlockSpec((1, RWKV_WIDTH), const),
                  pl.BlockSpec((1, RWKV_WIDTH), const)],
        out_specs=[pl.BlockSpec((tp, RWKV_WIDTH), row)] * len(out_dtypes),
        out_shape=[jax.ShapeDtypeStruct((m, RWKV_WIDTH), dt) for dt in out_dtypes],
        compiler_params=_cparams(1),
        name="prep",
    )(h, wfb, wa, wg, w0, a0, k_k, k_a, r_k)


def _block_diag(x, lane_lo):
    return jnp.concatenate([jnp.where(lane_lo, x, 0.0).astype(BF16),
                            jnp.where(lane_lo, 0.0, x).astype(BF16)], axis=0)


def _chunk_parallel(chains, common, out):
    eye2, lane_lo, _ = common
    bd = lambda x: _block_diag(x, lane_lo)
    n = len(chains)
    at, rt, kt, bt, g_all = [], [], [], [], []
    for (r, k, v, kk, b, lw, cum, _, bwd) in chains:
        gq = jnp.exp(cum)
        gk = jnp.exp(-cum)
        g_all.append(gq[0:1] if bwd else gq[CHUNK - 1:CHUNK])
        at.append(-(kk * jnp.exp(cum - lw)))
        rt.append(r * gq)
        kt.append(k * gk)
        bt.append(b * gk)
    yield

    a_ak, a_ab, a_rk, a_rb = [], [], [], []
    for c in range(n):
        tri_strict, tri_incl = chains[c][7]
        lhs = jnp.concatenate([at[c], rt[c]], axis=0).astype(BF16)
        rhs = jnp.concatenate([bd(kt[c]), bd(bt[c])], axis=0)
        big = _dot_nt(lhs, rhs)
        a_ak.append(jnp.where(tri_strict, big[0:CHUNK, 0:LANES], 0.0))
        a_ab.append(jnp.where(tri_strict, big[0:CHUNK, LANES:], 0.0))
        a_rk.append(jnp.where(tri_incl, big[CHUNK:, 0:LANES], 0.0))
        a_rb.append(jnp.where(tri_incl, big[CHUNK:, LANES:], 0.0))
    yield

    q = [eye2 + a for a in a_ab]
    p = [_dot(a.astype(BF16), bd(a)) for a in a_ab]
    xy = [_dot(jnp.concatenate([a_ak[c], a_rk[c]], axis=0).astype(BF16), bd(chains[c][2])) for c in range(n)]
    yield
    for _ in range(4):
        pq_pp = [_dot(p[c].astype(BF16), jnp.concatenate([bd(q[c]), bd(p[c])], axis=1)) for c in range(n)]
        q = [q[c] + pq_pp[c][:, 0:LANES] for c in range(n)]
        p = [pq_pp[c][:, LANES:] for c in range(n)]
        yield
    q = [q[c] + _dot(p[c].astype(BF16), bd(q[c])) for c in range(n)]
    yield
    wu = [_dot(q[c].astype(BF16), jnp.concatenate([bd(at[c]), bd(xy[c][0:CHUNK])], axis=1)) for c in range(n)]
    out.update(rt=rt, kt=kt, bt=bt, g_all=g_all, a_rb=a_rb, xy=xy, wu=wu, v=[ch[2] for ch in chains])


def _chunk_sequential(par, states, ys, common):
    _, lane_lo, bd_mask = common
    bd = lambda x: _block_diag(x, lane_lo)
    n = len(states)
    wu, rt, xy = par["wu"], par["rt"], par["xy"]
    s1 = [_dot_nt(jnp.concatenate([wu[c][:, 0:LANES], rt[c]], axis=0).astype(BF16), states[c].astype(BF16))
          for c in range(n)]
    u = [wu[c][:, LANES:] + s1[c][0:CHUNK] for c in range(n)]
    yield
    ys.extend(s1[c][CHUNK:] + xy[c][CHUNK:] + _dot(par["a_rb"][c].astype(BF16), bd(u[c])) for c in range(n))
    yield
    for c in range(n):
        vu_t = jnp.concatenate([par["v"][c], u[c]], axis=0).T.astype(BF16)
        kb = jnp.concatenate([par["kt"][c], par["bt"][c]], axis=0).astype(BF16)
        states[c] = jnp.where(bd_mask, (states[c] + _dot(vu_t, kb)) * par["g_all"][c], 0.0)
    yield


def _interleave(*gens):
    gens = list(gens)
    while gens:
        for g in list(gens):
            try:
                next(g)
            except StopIteration:
                gens.remove(g)


def _chain_gens(*gens):
    for g in gens:
        yield from g


def _scan_kernel(n_chunks,
                 rf_ref, kf_ref, vf_ref, kkf_ref, bf_ref, lwf_ref,
                 rb_ref, kb_ref, vb_ref, kkb_ref, bb_ref, lwb_ref,
                 yf_ref, yb_ref, state_ref, cumf_ref, cumb_ref):
    @pl.when(pl.program_id(1) == 0)
    def _():
        state_ref[...] = jnp.zeros_like(state_ref)

    n_pairs = kf_ref.shape[1] // LANES
    tb = n_chunks * CHUNK
    ri = lax.broadcasted_iota(jnp.int32, (CHUNK, LANES), 0)
    ci = lax.broadcasted_iota(jnp.int32, (CHUNK, LANES), 1) & (CHUNK - 1)
    lane_lo = lax.broadcasted_iota(jnp.int32, (CHUNK, LANES), 1) < CHUNK
    eye2 = jnp.where(ri == ci, 1.0, 0.0)
    br = lax.broadcasted_iota(jnp.int32, (LANES, LANES), 0) >> 6
    bc = lax.broadcasted_iota(jnp.int32, (LANES, LANES), 1) >> 6
    common = (eye2, lane_lo, br == bc)
    tri_f = (ci < ri, ci <= ri)
    tri_b = (ci > ri, ci >= ri)

    tr = lax.broadcasted_iota(jnp.int32, (tb, tb), 0)
    tc = lax.broadcasted_iota(jnp.int32, (tb, tb), 1)
    same = (tr >> 6) == (tc >> 6)
    ones_f = jnp.where(same & (tc <= tr), 1.0, 0.0).astype(BF16)
    ones_b = jnp.where(same & (tc >= tr), 1.0, 0.0).astype(BF16)

    def cumsum(ones, lw):
        hi, lo = _split_bf16(lw)
        return _dot(ones, hi) + _dot(ones, lo)

    cumf_ref[...] = cumsum(ones_f, lwf_ref[...])
    cumb_ref[...] = cumsum(ones_b, lwb_ref[...])

    def rows(step):
        return (slice(step * CHUNK, (step + 1) * CHUNK),
                slice((n_chunks - 1 - step) * CHUNK, (n_chunks - step) * CHUNK))

    def parallel(step, out):
        rows_f, rows_b = rows(step)
        chains = []
        for p in range(n_pairs):
            cols = slice(p * LANES, (p + 1) * LANES)

            def tok(ref, rws):
                if len(ref.shape) == 3:
                    per_tile = COL_TILE // LANES
                    lanes = slice((p % per_tile) * LANES, (p % per_tile + 1) * LANES)
                    return ref[p // per_tile, rws, lanes].astype(F32)
                return ref[rws, cols].astype(F32)

            tok_f = [tok(ref, rows_f) for ref in (rf_ref, kf_ref, vf_ref, kkf_ref, bf_ref)]
            tok_b = [tok(ref, rows_b) for ref in (rb_ref, kb_ref, vb_ref, kkb_ref, bb_ref)]
            chains.append((*tok_f, lwf_ref[rows_f, cols], cumf_ref[rows_f, cols], tri_f, False))
            chains.append((*tok_b, lwb_ref[rows_b, cols], cumb_ref[rows_b, cols], tri_b, True))
        yield from _chunk_parallel(chains, common, out)

    def sequential(step, par):
        rows_f, rows_b = rows(step)
        ys = []
        yield from _chunk_sequential(par, states, ys, common)
        for p in range(n_pairs):
            cols = slice(p * LANES, (p + 1) * LANES)
            yf_ref[rows_f, cols] = ys[2 * p].astype(yf_ref.dtype)
            yb_ref[rows_b, cols] = ys[2 * p + 1].astype(yb_ref.dtype)

    states = [state_ref[d, p] for p in range(n_pairs) for d in range(2)]
    pending = None
    for step in range(0, n_chunks, 2):
        par = [{}, {}]
        gens = [parallel(step, par[0]), parallel(step + 1, par[1])]
        if pending is not None:
            gens.append(_chain_gens(sequential(step - 2, pending[0]), sequential(step - 1, pending[1])))
        _interleave(*gens)
        pending = par
    _interleave(_chain_gens(sequential(n_chunks - 2, pending[0]), sequential(n_chunks - 1, pending[1])))
    for p in range(n_pairs):
        for d in range(2):
            state_ref[d, p] = states[2 * p + d]


def _scan(h, k, kk, b, lwf, lwb, seq):
    m = k.shape[0]
    n_chunks = 4
    tb = n_chunks * CHUNK
    nb = seq // tb
    group = RWKV_WIDTH // COL_TILE
    fwd = lambda s, i: (s * nb + i, 0)
    bwd = lambda s, i: (s * nb + nb - 1 - i, 0)
    blk = lambda im: pl.BlockSpec((tb, RWKV_WIDTH), im)
    hblk = lambda im, g: pl.BlockSpec((group, tb, COL_TILE), lambda s, i: (g, im(s, i)[0], 0))
    out = jax.ShapeDtypeStruct((m, RWKV_WIDTH), BF16)
    specs = lambda im: [hblk(im, 0), blk(im), hblk(im, 2), blk(im), blk(im), blk(im)]
    return pl.pallas_call(
        functools.partial(_scan_kernel, n_chunks),
        grid=(m // seq, nb),
        in_specs=specs(fwd) + specs(bwd),
        out_specs=[blk(fwd), blk(bwd)],
        out_shape=[out, out],
        scratch_shapes=[pltpu.VMEM((2, RWKV_WIDTH // LANES, LANES, LANES), F32),
                        pltpu.VMEM((tb, RWKV_WIDTH), F32),
                        pltpu.VMEM((tb, RWKV_WIDTH), F32)],
        compiler_params=_cparams(2),
        name="scan",
    )(h, k, h, kk, b, lwf, h, k, h, kk, b, lwb)


def _post_kernel(seq,
                 yf_ref, yb_ref, g_ref, bonus_ref, cv_ref, cvp_ref, cvn_ref,
                 lng_ref, lnb_ref, cw_ref, x_ref, gt_ref, w_ref, o_ref, mix_ref):
    tm = yf_ref.shape[0]
    t0 = (pl.program_id(0) * tm) % seq
    first = t0 == 0
    last = t0 + tm == seq
    y = yf_ref[...].astype(F32) + yb_ref[...].astype(F32)
    mean = _head_sum_bcast(y) * (1.0 / HEAD)
    d = y - mean
    var = _head_sum_bcast(d * d) * (1.0 / HEAD)
    yn = d * lax.rsqrt(var + LNX_EPS) * lng_ref[...] + lnb_ref[...]
    mix_ref[:, 0:RWKV_WIDTH] = ((yn + bonus_ref[...].astype(F32)) * g_ref[...].astype(F32)).astype(BF16)

    gate_z = lambda ref, rws=slice(None): (_tile_cols(ref, CONV_WIDTH, CONV_WIDTH, rws).astype(F32)
                                           * _tile_cols(ref, 2 * CONV_WIDTH, CONV_WIDTH, rws).astype(F32))
    z = gate_z(cv_ref)
    zp = jnp.where(first, 0.0, gate_z(cvp_ref)[HALO - 1:HALO])
    zn = jnp.where(last, 0.0, gate_z(cvn_ref)[0:1])
    ridx = lax.broadcasted_iota(jnp.int32, z.shape, 0)
    z_prev = jnp.where(ridx == 0, zp, pltpu.roll(z, 1, axis=0))
    z_next = jnp.where(ridx == tm - 1, zn, pltpu.roll(z, tm - 1, axis=0))
    cw = cw_ref[...]
    conv = z_prev * cw[0:1] + z * cw[1:2] + z_next * cw[2:3]
    mix_ref[:, RWKV_WIDTH:] = (_tile_cols(cv_ref, 0, CONV_WIDTH).astype(F32) * conv).astype(BF16)

    gt = gt_ref[0]
    for c0 in range(0, o_ref.shape[1], POST_OUT_COLS):
        cols = slice(c0, c0 + POST_OUT_COLS)
        o_ref[:, cols] = x_ref[:, cols] + gt[:, cols] * _dot(mix_ref[...], w_ref[:, cols])


def _post(yf, yb, g, bonus, h, lnx_g, lnx_b, conv_w, x2, gt1, w_out, seq):
    m, d = x2.shape
    tm = 512
    per_seq = seq // tm
    n_halo = m // HALO
    n_cv = h.shape[0]
    row = lambda i: (i, 0)
    const = lambda i: (0, 0)
    return pl.pallas_call(
        functools.partial(_post_kernel, seq),
        grid=(m // tm,),
        in_specs=[pl.BlockSpec((tm, RWKV_WIDTH), row)] * 4 + [
            pl.BlockSpec((n_cv, tm, COL_TILE), lambda i: (0, i, 0)),
            pl.BlockSpec((n_cv, HALO, COL_TILE), lambda i: (0, jnp.maximum(i * (tm // HALO) - 1, 0), 0)),
            pl.BlockSpec((n_cv, HALO, COL_TILE),
                         lambda i: (0, jnp.minimum((i + 1) * (tm // HALO), n_halo - 1), 0)),
            pl.BlockSpec((1, RWKV_WIDTH), const),
            pl.BlockSpec((1, RWKV_WIDTH), const),
            pl.BlockSpec((3, CONV_WIDTH), const),
            pl.BlockSpec((tm, d), row),
            pl.BlockSpec((1, 1, d), lambda i: (i // per_seq, 0, 0)),
            pl.BlockSpec((d, d), const, pipeline_mode=pl.Buffered(1))],
        out_specs=pl.BlockSpec((tm, d), row),
        out_shape=jax.ShapeDtypeStruct((m, d), F32),
        scratch_shapes=[pltpu.VMEM((tm, d), BF16)],
        compiler_params=_cparams(1),
        name="post",
    )(yf, yb, g, bonus, h, h, h, lnx_g, lnx_b, conv_w, x2, gt1, w_out)


def _ffn_kernel(n_split, x_ref, sc_ref, sh_ref, gt_ref, g2_ref, gf_ref, wg_ref, wu_ref, wd_ref, o_ref, hn_ref):
    j = pl.program_id(1)
    last = pl.num_programs(1) - 1
    tm, n_out = o_ref.shape
    rows = tm // n_split

    def swiglu_step(sl, init):
        hn = hn_ref[sl, :]
        gate = _dot(hn, wg_ref[...])
        up = _dot(hn, wu_ref[...])
        act = (gate * jax.nn.sigmoid(gate) * up).astype(BF16)
        for c0 in range(0, n_out, FFN_DOWN_COLS):
            cols = slice(c0, c0 + FFN_DOWN_COLS)
            part = _dot(act, wd_ref[:, cols])
            if init:
                o_ref[sl, cols] = part
            else:
                o_ref[sl, cols] += part

    @pl.when(j == 0)
    def _():
        for q in range(n_split):
            sl = slice(q * rows, (q + 1) * rows)
            x = x_ref[sl, :]
            ms = jnp.mean(x * x, axis=-1, keepdims=True)
            y = x * lax.rsqrt(ms + RMS_EPS) * g2_ref[...]
            hn_ref[sl, :] = (y * (1.0 + sc_ref[0]) + sh_ref[0]).astype(BF16)
            swiglu_step(sl, True)

    @pl.when((j > 0) & (j < last))
    def _():
        swiglu_step(slice(0, tm), False)

    @pl.when(j == last)
    def _():
        for q in range(n_split):
            sl = slice(q * rows, (q + 1) * rows)
            swiglu_step(sl, False)
            x = x_ref[sl, :] + gt_ref[0] * o_ref[sl, :]
            ms = jnp.mean(x * x, axis=-1, keepdims=True)
            o_ref[sl, :] = x * lax.rsqrt(ms + RMS_EPS) * gf_ref[...]


def _ffn(x1, sc, sh, gt, g2, gf, wg, wu, wd, seq):
    m, d = x1.shape
    tm, n_split = 1024, 4
    n_ff, _, tf = wg.shape
    per_seq = seq // tm
    row = lambda i, j: (i, 0)
    mod = lambda i, j: (i // per_seq, 0, 0)
    const = lambda i, j: (0, 0)
    return pl.pallas_call(
        functools.partial(_ffn_kernel, n_split),
        grid=(m // tm, n_ff),
        in_specs=[pl.BlockSpec((tm, d), row),
                  pl.BlockSpec((1, 1, d), mod),
                  pl.BlockSpec((1, 1, d), mod),
                  pl.BlockSpec((1, 1, d), mod),
                  pl.BlockSpec((1, d), const),
                  pl.BlockSpec((1, d), const),
                  pl.BlockSpec((None, d, tf), lambda i, j: (j, 0, 0)),
                  pl.BlockSpec((None, d, tf), lambda i, j: (j, 0, 0)),
                  pl.BlockSpec((tf, d), lambda i, j: (j, 0))],
        out_specs=pl.BlockSpec((tm, d), row),
        out_shape=jax.ShapeDtypeStruct((m, d), F32),
        scratch_shapes=[pltpu.VMEM((tm, d), BF16)],
        compiler_params=_cparams(2),
        name="ffn",
    )(x1, sc, sh, gt, g2, gf, wg, wu, wd)


def _layer_weights(w_in, mu_shift, w2_decay, a2, g2_gate):
    d = w_in.shape[0]
    n_rw = RKV_COLS + SMALL_COLS
    pad = RW_COLS - n_rw
    tile_major = lambda w: w.reshape(d, -1, COL_TILE).transpose(1, 0, 2)
    w_rw = tile_major(jnp.pad(w_in[:, :n_rw].astype(BF16), ((0, 0), (0, pad))))
    w_cv = tile_major(w_in[:, n_rw:].astype(BF16))
    mu_rw = jnp.pad(mu_shift, (0, pad)).reshape(1, RW_COLS)
    zeros = lambda rows, cols: jnp.zeros((rows, cols), F32)
    wfb = jnp.concatenate([
        jnp.concatenate([w2_decay[0], zeros(DECAY_RANK, RWKV_WIDTH)], axis=1),
        jnp.concatenate([zeros(DECAY_RANK, RWKV_WIDTH), w2_decay[1]], axis=1)], axis=0).astype(BF16)
    wa = jnp.concatenate([a2, zeros(LANES - AAA_RANK, RWKV_WIDTH)], axis=0).astype(BF16)
    g_rows = SMALL_PAD - LANES
    wg = jnp.concatenate([zeros(AAA_RANK, RWKV_WIDTH), g2_gate,
                          zeros(g_rows - AAA_RANK - GATE_RANK, RWKV_WIDTH)], axis=0).astype(BF16)
    return w_rw, w_cv, mu_rw, wfb, wa, wg


def _group(x, mod, lw):
    nb, seq, d = x.shape
    x2 = x.reshape(nb * seq, d)
    sh1, sc1, gt1, sh2, sc2, gt2 = [mod[:, q * d:(q + 1) * d].reshape(nb, 1, d) for q in range(6)]
    h, h_cv = _in_proj(x2, sc1, sh1, lw["norm1_g"], lw["w_rw"], lw["w_cv"], lw["mu_rw"], seq)
    k, kk, b, lwf, lwb, g, bonus = _prep(
        h, lw["wfb"], lw["wa"], lw["wg"], lw["w0"], lw["a0"], lw["k_k"], lw["k_a"], lw["r_k"])
    yf, yb = _scan(h, k, kk, b, lwf, lwb, seq)
    x1 = _post(yf, yb, g, bonus, h_cv, lw["lnx_g"], lw["lnx_b"], lw["conv_w"], x2, gt1, lw["w_out"], seq)
    out = _ffn(x1, sc2, sh2, gt2, lw["norm2_g"], lw["norm_f_g"], lw["wg_ffn"], lw["wu_ffn"], lw["wd_ffn"], seq)
    return out.reshape(nb, seq, d)


def kernel(x_prompt, x_sample, c_prompt, c_sample, w_ada, b_ada, norm1_g, w_in, mu_shift, w0_decay, w2_decay, a0, a2, g2_gate, k_k, k_a, r_k, lnx_g, lnx_b, conv_w, w_out, norm2_g, w_ffn_gate, w_ffn_up, w_ffn_down, norm_f_g):
    assert w_ada.shape[0] == 1, "single-layer block"
    l = 0
    row = lambda t: t.reshape(1, -1)
    w_rw, w_cv, mu_rw, wfb, wa, wg = _layer_weights(w_in[l], mu_shift[l], w2_decay[l], a2[l], g2_gate[l])
    ff_tiles = lambda w: w.astype(BF16).reshape(w.shape[0], -1, FFN_TILE).transpose(1, 0, 2)
    lw = dict(w_rw=w_rw, w_cv=w_cv, mu_rw=mu_rw, wfb=wfb, wa=wa, wg=wg,
              norm1_g=row(norm1_g[l]), w0=w0_decay[l], a0=row(a0[l]), k_k=row(k_k[l]), k_a=row(k_a[l]),
              r_k=row(r_k[l]), lnx_g=row(lnx_g[l]), lnx_b=row(lnx_b[l]), conv_w=conv_w[l],
              w_out=w_out[l].astype(BF16), norm2_g=row(norm2_g[l]), norm_f_g=row(norm_f_g),
              wg_ffn=ff_tiles(w_ffn_gate[l]), wu_ffn=ff_tiles(w_ffn_up[l]),
              wd_ffn=w_ffn_down[l].astype(BF16))
    n_prompt = c_prompt.shape[0]
    mod = _mod(jnp.concatenate([c_prompt, c_sample], axis=0), w_ada[l], b_ada[l])
    y_prompt = _group(x_prompt, mod[:n_prompt], lw)
    y_sample = _group(x_sample, mod[n_prompt:], lw)
    return (y_prompt, y_sample)
```

```python
import functools
import math

import jax
import jax.numpy as jnp
from jax import lax
from jax.experimental import pallas as pl
from jax.experimental.pallas import tpu as pltpu

F32 = jnp.float32
BF16 = jnp.bfloat16

D_MODEL = 2048
RWKV_WIDTH = 1024
CONV_WIDTH = 1024
HEAD = 64
DECAY_RANK = 64
AAA_RANK = 64
GATE_RANK = 160
SMALL_COLS = 2 * DECAY_RANK + AAA_RANK + GATE_RANK
SMALL_PAD = 512
RKV_COLS = 3 * RWKV_WIDTH
CONV_COLS = 3 * CONV_WIDTH
RW_COLS = RKV_COLS + SMALL_PAD
D_FF = 5632
RMS_EPS = 1e-6
LNX_EPS = 64e-5

CHUNK = 64
LANES = 128
MXU_COLS = 256
HALO = 16
COL_TILE = 512
FFN_TILE = 512
VMEM_LIMIT = 56 * 1024 * 1024
FFN_DOWN_COLS = 512
POST_OUT_COLS = 512


def _cparams(n_axes):
    return pltpu.CompilerParams(dimension_semantics=("arbitrary",) * n_axes,
                                vmem_limit_bytes=VMEM_LIMIT)


def _dot(a, b):
    return jnp.dot(a, b, preferred_element_type=F32)


def _dot_nt(a, b):
    return lax.dot_general(a, b, (((1,), (1,)), ((), ())), preferred_element_type=F32)


def _tile_cols(ref, c0, width, rows=slice(None)):
    tiles = range(c0 // COL_TILE, (c0 + width) // COL_TILE)
    return jnp.concatenate([ref[t, rows, :] for t in tiles], axis=1)


def _sigmoid(x):
    return 0.5 * jnp.tanh(0.5 * x) + 0.5


def _head_sum_bcast(x):
    n = MXU_COLS
    row = lax.broadcasted_iota(jnp.int32, (n, n), 0) >> 6
    col = lax.broadcasted_iota(jnp.int32, (n, n), 1) >> 6
    ones_bd = jnp.where(row == col, 1.0, 0.0).astype(BF16)
    outs = [_dot(x[:, q * n:(q + 1) * n].astype(BF16), ones_bd) for q in range(x.shape[1] // n)]
    return jnp.concatenate(outs, axis=1)


def _mod_kernel(c_ref, w_ref, b_ref, o_ref):
    c = c_ref[...]
    s = c * jax.nn.sigmoid(c)
    o_ref[...] = _dot(s.astype(BF16), w_ref[...].astype(BF16)) + b_ref[...]


def _mod(c, w_ada, b_ada):
    nb, d = c.shape
    n = w_ada.shape[1]
    tn = 1024
    return pl.pallas_call(
        _mod_kernel,
        grid=(n // tn,),
        in_specs=[pl.BlockSpec((nb, d), lambda j: (0, 0)),
                  pl.BlockSpec((d, tn), lambda j: (0, j)),
                  pl.BlockSpec((1, tn), lambda j: (0, j))],
        out_specs=pl.BlockSpec((nb, tn), lambda j: (0, j)),
        out_shape=jax.ShapeDtypeStruct((nb, n), F32),
        compiler_params=_cparams(1),
        name="mod",
    )(c, w_ada, b_ada.reshape(1, n))


def _token_shift_mix(h, prev_row, next_row, mu):
    rows, cols = h.shape
    r8 = lax.broadcasted_iota(jnp.int32, (8, cols), 0)
    even = (lax.broadcasted_iota(jnp.int32, (rows, cols), 1) & 1) == 0
    prv = pltpu.roll(h, 1, axis=0)
    prv = jnp.concatenate([jnp.where(r8 == 0, prev_row, prv[0:8]), prv[8:]], axis=0)
    nxt = pltpu.roll(h, rows - 1, axis=0)
    nxt = jnp.concatenate([nxt[:rows - 8], jnp.where(r8 == 7, next_row, nxt[rows - 8:])], axis=0)
    return h + mu * (jnp.where(even, prv, nxt) - h)


def _inproj_kernel(seq, n_split, x_ref, xp_ref, xn_ref, sc_ref, sh_ref, g_ref, wrw_ref, wcv_ref, mu_ref,
                   orw_ref, ocv_ref, hn_ref):
    tm = x_ref.shape[0]
    tn = orw_ref.shape[1]
    t0 = (pl.program_id(0) * tm) % seq
    j = pl.program_id(1)

    def normed(x):
        ms = jnp.mean(x * x, axis=-1, keepdims=True)
        y = x * lax.rsqrt(ms + RMS_EPS) * g_ref[...]
        return y * (1.0 + sc_ref[0]) + sh_ref[0]

    def mix_store(h_all, cols):
        prev_row = jnp.where(t0 == 0, 0.0, h_all[tm + 7:tm + 8])
        next_row = jnp.where(t0 + tm == seq, 0.0, h_all[tm + 8:tm + 9])
        orw_ref[:, cols] = _token_shift_mix(h_all[0:tm], prev_row, next_row, mu_ref[:, cols]).astype(orw_ref.dtype)

    def conv_tile():
        ocv_ref[...] = _dot(hn_ref[0:tm, :], wcv_ref[...]).astype(ocv_ref.dtype)

    def rwkv_tile():
        for c0 in range(0, tn, MXU_COLS):
            cols = slice(c0, c0 + MXU_COLS)
            mix_store(_dot(hn_ref[...], wrw_ref[:, cols]), cols)

    @pl.when(j == 0)
    def _():
        hn_ref[tm:, :] = jnp.concatenate([normed(xp_ref[...]), normed(xn_ref[...])], axis=0).astype(BF16)
        rows = tm // n_split
        parts = []
        for q in range(n_split):
            sl = slice(q * rows, (q + 1) * rows)
            hn_ref[sl, :] = normed(x_ref[sl, :]).astype(BF16)
            sl = slice(q * rows, (q + 1) * rows if q + 1 < n_split else tm + 16)
            parts.append(_dot(hn_ref[sl, :], wrw_ref[...]))
        h_all = jnp.concatenate(parts, axis=0)
        for c0 in range(0, tn, MXU_COLS):
            mix_store(h_all[:, c0:c0 + MXU_COLS], slice(c0, c0 + MXU_COLS))

    @pl.when(j > 0)
    def _():
        rwkv_tile()
        conv_tile()


def _in_proj(x2, sc, sh, g1, w_rw, w_cv, mu_rw, seq):
    m, d = x2.shape
    tm, tn, n_split = 1024, COL_TILE, 4
    per_seq = seq // tm
    nblk8 = m // 8
    n_rw, n_conv = w_rw.shape[0], w_cv.shape[0]
    assert n_rw == n_conv + 1
    mod = lambda i, j: (i // per_seq, 0, 0)
    rw_tile = lambda j: (j + n_conv) % n_rw
    conv_tile = lambda j: jnp.maximum(j - 1, 0)
    return pl.pallas_call(
        functools.partial(_inproj_kernel, seq, n_split),
        grid=(m // tm, n_rw),
        in_specs=[pl.BlockSpec((tm, d), lambda i, j: (i, 0)),
                  pl.BlockSpec((8, d), lambda i, j: (jnp.maximum(i * (tm // 8) - 1, 0), 0)),
                  pl.BlockSpec((8, d), lambda i, j: (jnp.minimum((i + 1) * (tm // 8), nblk8 - 1), 0)),
                  pl.BlockSpec((1, 1, d), mod),
                  pl.BlockSpec((1, 1, d), mod),
                  pl.BlockSpec((1, d), lambda i, j: (0, 0)),
                  pl.BlockSpec((None, d, tn), lambda i, j: (rw_tile(j), 0, 0)),
                  pl.BlockSpec((None, d, tn), lambda i, j: (conv_tile(j), 0, 0)),
                  pl.BlockSpec((1, tn), lambda i, j: (0, rw_tile(j)))],
        out_specs=[pl.BlockSpec((None, tm, tn), lambda i, j: (rw_tile(j), i, 0)),
                   pl.BlockSpec((None, tm, tn), lambda i, j: (conv_tile(j), i, 0))],
        out_shape=[jax.ShapeDtypeStruct((n_rw, m, tn), BF16), jax.ShapeDtypeStruct((n_conv, m, tn), BF16)],
        scratch_shapes=[pltpu.VMEM((tm + 16, d), BF16)],
        compiler_params=_cparams(2),
        name="in_proj",
    )(x2, x2, x2, sc, sh, g1, w_rw, w_cv, mu_rw)


def _prep_kernel(h_ref, wfb_ref, wa_ref, wg_ref, tri_ref,
                 w0_ref, a0_ref, kk_ref, ka_ref, rk_ref,
                 k_out, kk_out, b_out, cumf_out, cumb_out, g_out, bonus_out):
    xs_sm = h_ref[RKV_COLS // COL_TILE].astype(F32)
    lora_fb = _dot(jnp.tanh(xs_sm[:, 0:LANES]).astype(BF16), wfb_ref[...])
    lora_a = _dot(xs_sm[:, LANES:2 * LANES].astype(BF16), wa_ref[...])
    g = _dot(_sigmoid(xs_sm[:, LANES:]).astype(BF16), wg_ref[...])

    half_scale = 0.5 * math.exp(-0.5)
    w0 = w0_ref[...]
    lw_f = -half_scale * (1.0 + jnp.tanh(0.5 * (w0[0:1] + lora_fb[:, :RWKV_WIDTH])))
    lw_b = -half_scale * (1.0 + jnp.tanh(0.5 * (w0[1:2] + lora_fb[:, RWKV_WIDTH:])))
    span = tri_ref.shape[1]
    for r0 in range(0, lw_f.shape[0], span):
        rows = slice(r0, r0 + span)
        cumf_out[rows, :] = _dot(tri_ref[0], lw_f[rows].astype(BF16))
        cumb_out[rows, :] = _dot(tri_ref[1], lw_b[rows].astype(BF16))
    g_out[...] = g.astype(g_out.dtype)
    a = _sigmoid(a0_ref[...] + lora_a)

    xk = _tile_cols(h_ref, RWKV_WIDTH, RWKV_WIDTH).astype(F32)
    kk_raw = xk * kk_ref[...]
    ss = _head_sum_bcast(kk_raw * kk_raw)
    kk = kk_raw / jnp.maximum(jnp.sqrt(ss), 1e-12)
    k = xk * (1.0 + (a - 1.0) * ka_ref[...])
    kk_out[...] = kk.astype(kk_out.dtype)
    b_out[...] = (kk * a).astype(b_out.dtype)
    k_out[...] = k.astype(k_out.dtype)
    r = _tile_cols(h_ref, 0, RWKV_WIDTH).astype(F32)
    rk_sum = _head_sum_bcast(r * k * rk_ref[...])
    v = _tile_cols(h_ref, 2 * RWKV_WIDTH, RWKV_WIDTH).astype(F32)
    bonus_out[...] = (rk_sum * v).astype(bonus_out.dtype)


def _prep(h, wfb, wa, wg, w0, a0, k_k, k_a, r_k):
    n_tiles, m, _ = h.shape
    tp = 512
    row = lambda i: (i, 0)
    const = lambda i: (0, 0)
    t = jnp.arange(MXU_COLS)
    same_chunk = (t[:, None] // CHUNK) == (t[None, :] // CHUNK)
    tri = jnp.stack([same_chunk & (t[None, :] <= t[:, None]),
                     same_chunk & (t[None, :] >= t[:, None])]).astype(BF16)
    out_dtypes = [BF16] * 3 + [F32, F32, BF16, BF16]
    return pl.pallas_call(
        _prep_kernel,
        grid=(m // tp,),
        in_specs=[pl.BlockSpec((n_tiles, tp, COL_TILE), lambda i: (0, i, 0)),
                  pl.BlockSpec(wfb.shape, const),
                  pl.BlockSpec(wa.shape, const),
                  pl.BlockSpec(wg.shape, const),
                  pl.BlockSpec(tri.shape, lambda i: (0, 0, 0)),
                  pl.BlockSpec((2, RWKV_WIDTH), const),
                  pl.BlockSpec((1, RWKV_WIDTH), const),
                  pl.BlockSpec((1, RWKV_WIDTH), const),
                  pl.BlockSpec((1, RWKV_WIDTH), const),
                  pl.BlockSpec((1, RWKV_WIDTH), const)],
        out_specs=[pl.BlockSpec((tp, RWKV_WIDTH), row)] * len(out_dtypes),
        out_shape=[jax.ShapeDtypeStruct((m, RWKV_WIDTH), dt) for dt in out_dtypes],
        compiler_params=_cparams(1),
        name="prep",
    )(h, wfb, wa, wg, tri, w0, a0, k_k, k_a, r_k)


def _block_diag(x, lane_lo):
    return jnp.concatenate([jnp.where(lane_lo, x, 0.0).astype(BF16),
                            jnp.where(lane_lo, 0.0, x).astype(BF16)], axis=0)


def _chunk_parallel(chains, common, out):
    eye2, lane_lo, _ = common
    bd = lambda x: _block_diag(x, lane_lo)
    n = len(chains)
    row = lax.broadcasted_iota(jnp.int32, (CHUNK, LANES), 0)
    at, rt, kt, bt, g_all = [], [], [], [], []
    for (r, k, v, kk, b, cum, _, bwd) in chains:
        gq = jnp.exp(cum)
        gk = jnp.exp(-cum)
        g_all.append(gq[0:1] if bwd else gq[CHUNK - 1:CHUNK])
        if bwd:
            excl = jnp.where(row == CHUNK - 1, 0.0, pltpu.roll(cum, CHUNK - 1, axis=0))
        else:
            excl = jnp.where(row == 0, 0.0, pltpu.roll(cum, 1, axis=0))
        at.append(-(kk * jnp.exp(excl)))
        rt.append(r * gq)
        kt.append(k * gk)
        bt.append(b * gk)
    yield

    a_ak, a_ab, a_rk, a_rb = [], [], [], []
    for c in range(n):
        tri_strict, tri_incl = chains[c][6]
        lhs = jnp.concatenate([at[c], rt[c]], axis=0).astype(BF16)
        rhs = jnp.concatenate([bd(kt[c]), bd(bt[c])], axis=0)
        big = _dot_nt(lhs, rhs)
        a_ak.append(jnp.where(tri_strict, big[0:CHUNK, 0:LANES], 0.0))
        a_ab.append(jnp.where(tri_strict, big[0:CHUNK, LANES:], 0.0))
        a_rk.append(jnp.where(tri_incl, big[CHUNK:, 0:LANES], 0.0))
        a_rb.append(jnp.where(tri_incl, big[CHUNK:, LANES:], 0.0))
    yield

    q = [eye2 + a for a in a_ab]
    p = [_dot(a.astype(BF16), bd(a)) for a in a_ab]
    xy = [_dot(jnp.concatenate([a_ak[c], a_rk[c]], axis=0).astype(BF16), bd(chains[c][2])) for c in range(n)]
    yield
    for _ in range(4):
        pq_pp = [_dot(p[c].astype(BF16), jnp.concatenate([bd(q[c]), bd(p[c])], axis=1)) for c in range(n)]
        q = [q[c] + pq_pp[c][:, 0:LANES] for c in range(n)]
        p = [pq_pp[c][:, LANES:] for c in range(n)]
        yield
    q = [q[c] + _dot(p[c].astype(BF16), bd(q[c])) for c in range(n)]
    yield
    wu = [_dot(q[c].astype(BF16), jnp.concatenate([bd(at[c]), bd(xy[c][0:CHUNK])], axis=1)) for c in range(n)]
    out.update(rt=rt, kt=kt, bt=bt, g_all=g_all, a_rb=a_rb, xy=xy, wu=wu, v=[ch[2] for ch in chains])


def _chunk_sequential(par, states, ys, common):
    _, lane_lo, bd_mask = common
    bd = lambda x: _block_diag(x, lane_lo)
    n = len(states)
    wu, rt, xy = par["wu"], par["rt"], par["xy"]
    s1 = [_dot_nt(jnp.concatenate([wu[c][:, 0:LANES], rt[c]], axis=0).astype(BF16), states[c].astype(BF16))
          for c in range(n)]
    u = [wu[c][:, LANES:] + s1[c][0:CHUNK] for c in range(n)]
    yield
    ys.extend(s1[c][CHUNK:] + xy[c][CHUNK:] + _dot(par["a_rb"][c].astype(BF16), bd(u[c])) for c in range(n))
    yield
    for c in range(n):
        vu_t = jnp.concatenate([par["v"][c], u[c]], axis=0).T.astype(BF16)
        kb = jnp.concatenate([par["kt"][c], par["bt"][c]], axis=0).astype(BF16)
        states[c] = jnp.where(bd_mask, (states[c] + _dot(vu_t, kb)) * par["g_all"][c], 0.0)
    yield


def _interleave(*gens):
    gens = list(gens)
    while gens:
        for g in list(gens):
            try:
                next(g)
            except StopIteration:
                gens.remove(g)


def _chain_gens(*gens):
    for g in gens:
        yield from g


def _scan_kernel(n_chunks,
                 rf_ref, kf_ref, vf_ref, kkf_ref, bf_ref, cumf_ref,
                 rb_ref, kb_ref, vb_ref, kkb_ref, bb_ref, cumb_ref,
                 yf_ref, yb_ref, state_ref):
    @pl.when(pl.program_id(1) == 0)
    def _():
        state_ref[...] = jnp.zeros_like(state_ref)

    n_pairs = kf_ref.shape[1] // LANES
    ri = lax.broadcasted_iota(jnp.int32, (CHUNK, LANES), 0)
    ci = lax.broadcasted_iota(jnp.int32, (CHUNK, LANES), 1) & (CHUNK - 1)
    lane_lo = lax.broadcasted_iota(jnp.int32, (CHUNK, LANES), 1) < CHUNK
    eye2 = jnp.where(ri == ci, 1.0, 0.0)
    br = lax.broadcasted_iota(jnp.int32, (LANES, LANES), 0) >> 6
    bc = lax.broadcasted_iota(jnp.int32, (LANES, LANES), 1) >> 6
    common = (eye2, lane_lo, br == bc)
    tri_f = (ci < ri, ci <= ri)
    tri_b = (ci > ri, ci >= ri)

    def rows(step):
        return (slice(step * CHUNK, (step + 1) * CHUNK),
                slice((n_chunks - 1 - step) * CHUNK, (n_chunks - step) * CHUNK))

    def parallel(step, out):
        rows_f, rows_b = rows(step)
        chains = []
        for p in range(n_pairs):
            cols = slice(p * LANES, (p + 1) * LANES)

            def tok(ref, rws):
                if len(ref.shape) == 3:
                    per_tile = COL_TILE // LANES
                    lanes = slice((p % per_tile) * LANES, (p % per_tile + 1) * LANES)
                    return ref[p // per_tile, rws, lanes].astype(F32)
                return ref[rws, cols].astype(F32)

            tok_f = [tok(ref, rows_f) for ref in (rf_ref, kf_ref, vf_ref, kkf_ref, bf_ref)]
            tok_b = [tok(ref, rows_b) for ref in (rb_ref, kb_ref, vb_ref, kkb_ref, bb_ref)]
            chains.append((*tok_f, cumf_ref[rows_f, cols], tri_f, False))
            chains.append((*tok_b, cumb_ref[rows_b, cols], tri_b, True))
        yield from _chunk_parallel(chains, common, out)

    def sequential(step, par):
        rows_f, rows_b = rows(step)
        ys = []
        yield from _chunk_sequential(par, states, ys, common)
        for p in range(n_pairs):
            cols = slice(p * LANES, (p + 1) * LANES)
            yf_ref[rows_f, cols] = ys[2 * p].astype(yf_ref.dtype)
            yb_ref[rows_b, cols] = ys[2 * p + 1].astype(yb_ref.dtype)

    states = [state_ref[d, p] for p in range(n_pairs) for d in range(2)]
    pending = None
    for step in range(0, n_chunks, 2):
        par = [{}, {}]
        gens = [parallel(step, par[0]), parallel(step + 1, par[1])]
        if pending is not None:
            gens.append(_chain_gens(sequential(step - 2, pending[0]), sequential(step - 1, pending[1])))
        _interleave(*gens)
        pending = par
    _interleave(_chain_gens(sequential(n_chunks - 2, pending[0]), sequential(n_chunks - 1, pending[1])))
    for p in range(n_pairs):
        for d in range(2):
            state_ref[d, p] = states[2 * p + d]


def _scan(h, k, kk, b, cumf, cumb, seq):
    m = k.shape[0]
    n_chunks = 4
    tb = n_chunks * CHUNK
    nb = seq // tb
    group = RWKV_WIDTH // COL_TILE
    fwd = lambda s, i: (s * nb + i, 0)
    bwd = lambda s, i: (s * nb + nb - 1 - i, 0)
    blk = lambda im: pl.BlockSpec((tb, RWKV_WIDTH), im)
    hblk = lambda im, g: pl.BlockSpec((group, tb, COL_TILE), lambda s, i: (g, im(s, i)[0], 0))
    out = jax.ShapeDtypeStruct((m, RWKV_WIDTH), BF16)
    specs = lambda im: [hblk(im, 0), blk(im), hblk(im, 2), blk(im), blk(im), blk(im)]
    return pl.pallas_call(
        functools.partial(_scan_kernel, n_chunks),
        grid=(m // seq, nb),
        in_specs=specs(fwd) + specs(bwd),
        out_specs=[blk(fwd), blk(bwd)],
        out_shape=[out, out],
        scratch_shapes=[pltpu.VMEM((2, RWKV_WIDTH // LANES, LANES, LANES), F32)],
        compiler_params=_cparams(2),
        name="scan",
    )(h, k, h, kk, b, cumf, h, k, h, kk, b, cumb)


def _post_kernel(seq,
                 yf_ref, yb_ref, g_ref, bonus_ref, cv_ref, cvp_ref, cvn_ref,
                 lng_ref, lnb_ref, cw_ref, x_ref, gt_ref, w_ref, o_ref, mix_ref):
    tm = yf_ref.shape[0]
    t0 = (pl.program_id(0) * tm) % seq
    first = t0 == 0
    last = t0 + tm == seq
    y = yf_ref[...].astype(F32) + yb_ref[...].astype(F32)
    mean = _head_sum_bcast(y) * (1.0 / HEAD)
    d = y - mean
    var = _head_sum_bcast(d * d) * (1.0 / HEAD)
    yn = d * lax.rsqrt(var + LNX_EPS) * lng_ref[...] + lnb_ref[...]
    mix_ref[:, 0:RWKV_WIDTH] = ((yn + bonus_ref[...].astype(F32)) * g_ref[...].astype(F32)).astype(BF16)

    gate_z = lambda ref, rws=slice(None): (_tile_cols(ref, CONV_WIDTH, CONV_WIDTH, rws).astype(F32)
                                           * _tile_cols(ref, 2 * CONV_WIDTH, CONV_WIDTH, rws).astype(F32))
    z = gate_z(cv_ref)
    zp = jnp.where(first, 0.0, gate_z(cvp_ref)[HALO - 1:HALO])
    zn = jnp.where(last, 0.0, gate_z(cvn_ref)[0:1])
    ridx = lax.broadcasted_iota(jnp.int32, z.shape, 0)
    z_prev = jnp.where(ridx == 0, zp, pltpu.roll(z, 1, axis=0))
    z_next = jnp.where(ridx == tm - 1, zn, pltpu.roll(z, tm - 1, axis=0))
    cw = cw_ref[...]
    conv = z_prev * cw[0:1] + z * cw[1:2] + z_next * cw[2:3]
    mix_ref[:, RWKV_WIDTH:] = (_tile_cols(cv_ref, 0, CONV_WIDTH).astype(F32) * conv).astype(BF16)

    gt = gt_ref[0]
    for c0 in range(0, o_ref.shape[1], POST_OUT_COLS):
        cols = slice(c0, c0 + POST_OUT_COLS)
        o_ref[:, cols] = x_ref[:, cols] + gt[:, cols] * _dot(mix_ref[...], w_ref[:, cols])


def _post(yf, yb, g, bonus, h, lnx_g, lnx_b, conv_w, x2, gt1, w_out, seq):
    m, d = x2.shape
    tm = 512
    per_seq = seq // tm
    n_halo = m // HALO
    n_cv = h.shape[0]
    row = lambda i: (i, 0)
    const = lambda i: (0, 0)
    return pl.pallas_call(
        functools.partial(_post_kernel, seq),
        grid=(m // tm,),
        in_specs=[pl.BlockSpec((tm, RWKV_WIDTH), row)] * 4 + [
            pl.BlockSpec((n_cv, tm, COL_TILE), lambda i: (0, i, 0)),
            pl.BlockSpec((n_cv, HALO, COL_TILE), lambda i: (0, jnp.maximum(i * (tm // HALO) - 1, 0), 0)),
            pl.BlockSpec((n_cv, HALO, COL_TILE),
                         lambda i: (0, jnp.minimum((i + 1) * (tm // HALO), n_halo - 1), 0)),
            pl.BlockSpec((1, RWKV_WIDTH), const),
            pl.BlockSpec((1, RWKV_WIDTH), const),
            pl.BlockSpec((3, CONV_WIDTH), const),
            pl.BlockSpec((tm, d), row),
            pl.BlockSpec((1, 1, d), lambda i: (i // per_seq, 0, 0)),
            pl.BlockSpec((d, d), const, pipeline_mode=pl.Buffered(1))],
        out_specs=pl.BlockSpec((tm, d), row),
        out_shape=jax.ShapeDtypeStruct((m, d), F32),
        scratch_shapes=[pltpu.VMEM((tm, d), BF16)],
        compiler_params=_cparams(1),
        name="post",
    )(yf, yb, g, bonus, h, h, h, lnx_g, lnx_b, conv_w, x2, gt1, w_out)


def _ffn_kernel(n_split, x_ref, sc_ref, sh_ref, gt_ref, g2_ref, gf_ref, wg_ref, wu_ref, wd_ref, o_ref, hn_ref):
    j = pl.program_id(1)
    last = pl.num_programs(1) - 1
    tm, n_out = o_ref.shape
    rows = tm // n_split

    def swiglu_step(sl, init):
        hn = hn_ref[sl, :]
        gate = _dot(hn, wg_ref[...])
        up = _dot(hn, wu_ref[...])
        act = (gate * jax.nn.sigmoid(gate) * up).astype(BF16)
        for c0 in range(0, n_out, FFN_DOWN_COLS):
            cols = slice(c0, c0 + FFN_DOWN_COLS)
            part = _dot(act, wd_ref[:, cols])
            if init:
                o_ref[sl, cols] = part
            else:
                o_ref[sl, cols] += part

    @pl.when(j == 0)
    def _():
        for q in range(n_split):
            sl = slice(q * rows, (q + 1) * rows)
            x = x_ref[sl, :]
            ms = jnp.mean(x * x, axis=-1, keepdims=True)
            y = x * lax.rsqrt(ms + RMS_EPS) * g2_ref[...]
            hn_ref[sl, :] = (y * (1.0 + sc_ref[0]) + sh_ref[0]).astype(BF16)
            swiglu_step(sl, True)

    @pl.when((j > 0) & (j < last))
    def _():
        swiglu_step(slice(0, tm), False)

    @pl.when(j == last)
    def _():
        for q in range(n_split):
            sl = slice(q * rows, (q + 1) * rows)
            swiglu_step(sl, False)
            x = x_ref[sl, :] + gt_ref[0] * o_ref[sl, :]
            ms = jnp.mean(x * x, axis=-1, keepdims=True)
            o_ref[sl, :] = x * lax.rsqrt(ms + RMS_EPS) * gf_ref[...]


def _ffn(x1, sc, sh, gt, g2, gf, wg, wu, wd, seq):
    m, d = x1.shape
    tm, tf, n_split = 1024, FFN_TILE, 4
    n_ff = wg.shape[1] // tf
    per_seq = seq // tm
    row = lambda i, j: (i, 0)
    mod = lambda i, j: (i // per_seq, 0, 0)
    const = lambda i, j: (0, 0)
    return pl.pallas_call(
        functools.partial(_ffn_kernel, n_split),
        grid=(m // tm, n_ff),
        in_specs=[pl.BlockSpec((tm, d), row),
                  pl.BlockSpec((1, 1, d), mod),
                  pl.BlockSpec((1, 1, d), mod),
                  pl.BlockSpec((1, 1, d), mod),
                  pl.BlockSpec((1, d), const),
                  pl.BlockSpec((1, d), const),
                  pl.BlockSpec((d, tf), lambda i, j: (0, j)),
                  pl.BlockSpec((d, tf), lambda i, j: (0, j)),
                  pl.BlockSpec((tf, d), lambda i, j: (j, 0))],
        out_specs=pl.BlockSpec((tm, d), row),
        out_shape=jax.ShapeDtypeStruct((m, d), F32),
        scratch_shapes=[pltpu.VMEM((tm, d), BF16)],
        compiler_params=_cparams(2),
        name="ffn",
    )(x1, sc, sh, gt, g2, gf, wg, wu, wd)


def _layer_weights(w_in, mu_shift, w2_decay, a2, g2_gate):
    d = w_in.shape[0]
    n_rw = RKV_COLS + SMALL_COLS
    pad = RW_COLS - n_rw
    tile_major = lambda w: w.reshape(d, -1, COL_TILE).transpose(1, 0, 2)
    w_rw = tile_major(jnp.pad(w_in[:, :n_rw].astype(BF16), ((0, 0), (0, pad))))
    w_cv = tile_major(w_in[:, n_rw:].astype(BF16))
    mu_rw = jnp.pad(mu_shift, (0, pad)).reshape(1, RW_COLS)
    zeros = lambda rows, cols: jnp.zeros((rows, cols), F32)
    wfb = jnp.concatenate([
        jnp.concatenate([w2_decay[0], zeros(DECAY_RANK, RWKV_WIDTH)], axis=1),
        jnp.concatenate([zeros(DECAY_RANK, RWKV_WIDTH), w2_decay[1]], axis=1)], axis=0).astype(BF16)
    wa = jnp.concatenate([a2, zeros(LANES - AAA_RANK, RWKV_WIDTH)], axis=0).astype(BF16)
    g_rows = SMALL_PAD - LANES
    wg = jnp.concatenate([zeros(AAA_RANK, RWKV_WIDTH), g2_gate,
                          zeros(g_rows - AAA_RANK - GATE_RANK, RWKV_WIDTH)], axis=0).astype(BF16)
    return w_rw, w_cv, mu_rw, wfb, wa, wg


def _group(x, mod, lw):
    nb, seq, d = x.shape
    x2 = x.reshape(nb * seq, d)
    sh1, sc1, gt1, sh2, sc2, gt2 = [mod[:, q * d:(q + 1) * d].reshape(nb, 1, d) for q in range(6)]
    h, h_cv = _in_proj(x2, sc1, sh1, lw["norm1_g"], lw["w_rw"], lw["w_cv"], lw["mu_rw"], seq)
    k, kk, b, cumf, cumb, g, bonus = _prep(
        h, lw["wfb"], lw["wa"], lw["wg"], lw["w0"], lw["a0"], lw["k_k"], lw["k_a"], lw["r_k"])
    yf, yb = _scan(h, k, kk, b, cumf, cumb, seq)
    x1 = _post(yf, yb, g, bonus, h_cv, lw["lnx_g"], lw["lnx_b"], lw["conv_w"], x2, gt1, lw["w_out"], seq)
    out = _ffn(x1, sc2, sh2, gt2, lw["norm2_g"], lw["norm_f_g"], lw["wg_ffn"], lw["wu_ffn"], lw["wd_ffn"], seq)
    return out.reshape(nb, seq, d)


def kernel(x_prompt, x_sample, c_prompt, c_sample, w_ada, b_ada, norm1_g, w_in, mu_shift, w0_decay, w2_decay, a0, a2, g2_gate, k_k, k_a, r_k, lnx_g, lnx_b, conv_w, w_out, norm2_g, w_ffn_gate, w_ffn_up, w_ffn_down, norm_f_g):
    assert w_ada.shape[0] == 1, "single-layer block"
    l = 0
    row = lambda t: t.reshape(1, -1)
    w_rw, w_cv, mu_rw, wfb, wa, wg = _layer_weights(w_in[l], mu_shift[l], w2_decay[l], a2[l], g2_gate[l])
    lw = dict(w_rw=w_rw, w_cv=w_cv, mu_rw=mu_rw, wfb=wfb, wa=wa, wg=wg,
              norm1_g=row(norm1_g[l]), w0=w0_decay[l], a0=row(a0[l]), k_k=row(k_k[l]), k_a=row(k_a[l]),
              r_k=row(r_k[l]), lnx_g=row(lnx_g[l]), lnx_b=row(lnx_b[l]), conv_w=conv_w[l],
              w_out=w_out[l].astype(BF16), norm2_g=row(norm2_g[l]), norm_f_g=row(norm_f_g),
              wg_ffn=w_ffn_gate[l].astype(BF16), wu_ffn=w_ffn_up[l].astype(BF16),
              wd_ffn=w_ffn_down[l].astype(BF16))
    n_prompt = c_prompt.shape[0]
    mod = _mod(jnp.concatenate([c_prompt, c_sample], axis=0), w_ada[l], b_ada[l])
    y_prompt = _group(x_prompt, mod[:n_prompt], lw)
    y_sample = _group(x_sample, mod[n_prompt:], lw)
    return (y_prompt, y_sample)
```

```python
import functools
import math

import jax
import jax.numpy as jnp
from jax import lax
from jax.experimental import pallas as pl
from jax.experimental.pallas import tpu as pltpu

F32 = jnp.float32
BF16 = jnp.bfloat16

D_MODEL = 2048
RWKV_WIDTH = 1024
CONV_WIDTH = 1024
HEAD = 64
DECAY_RANK = 64
AAA_RANK = 64
GATE_RANK = 160
SMALL_COLS = 2 * DECAY_RANK + AAA_RANK + GATE_RANK
SMALL_PAD = 512
RKV_COLS = 3 * RWKV_WIDTH
CONV_COLS = 3 * CONV_WIDTH
RW_COLS = RKV_COLS + SMALL_PAD
D_FF = 5632
RMS_EPS = 1e-6
LNX_EPS = 64e-5

CHUNK = 64
LANES = 128
MXU_COLS = 256
HALO = 16
COL_TILE = 512
FFN_TILE = 512
MATMUL_ROWS = 1024
NORM_SPLIT = 4
ELEMENTWISE_ROWS = 512
SCAN_CHUNKS = 4
MOD_COLS = 1024
VMEM_LIMIT = 56 * 1024 * 1024
FFN_DOWN_COLS = 512
POST_OUT_COLS = 512


def _cparams(n_axes):
    return pltpu.CompilerParams(dimension_semantics=("arbitrary",) * n_axes,
                                vmem_limit_bytes=VMEM_LIMIT)


def _dot(a, b):
    return jnp.dot(a, b, preferred_element_type=F32)


def _dot_nt(a, b):
    return lax.dot_general(a, b, (((1,), (1,)), ((), ())), preferred_element_type=F32)


def _tile_cols(ref, c0, width, rows=slice(None)):
    tiles = range(c0 // COL_TILE, (c0 + width) // COL_TILE)
    return jnp.concatenate([ref[t, rows, :] for t in tiles], axis=1)


def _sigmoid(x):
    return 0.5 * jnp.tanh(0.5 * x) + 0.5


def _head_sum_bcast(x):
    n = MXU_COLS
    row = lax.broadcasted_iota(jnp.int32, (n, n), 0) >> 6
    col = lax.broadcasted_iota(jnp.int32, (n, n), 1) >> 6
    ones_bd = jnp.where(row == col, 1.0, 0.0).astype(BF16)
    outs = [_dot(x[:, q * n:(q + 1) * n].astype(BF16), ones_bd) for q in range(x.shape[1] // n)]
    return jnp.concatenate(outs, axis=1)


def _mod_kernel(c_ref, w_ref, b_ref, o_ref):
    c = c_ref[...]
    s = c * jax.nn.sigmoid(c)
    o_ref[...] = _dot(s.astype(BF16), w_ref[...].astype(BF16)) + b_ref[...]


def _mod(c, w_ada, b_ada):
    nb, d = c.shape
    n = w_ada.shape[1]
    tn = MOD_COLS
    return pl.pallas_call(
        _mod_kernel,
        grid=(n // tn,),
        in_specs=[pl.BlockSpec((nb, d), lambda j: (0, 0)),
                  pl.BlockSpec((d, tn), lambda j: (0, j)),
                  pl.BlockSpec((1, tn), lambda j: (0, j))],
        out_specs=pl.BlockSpec((nb, tn), lambda j: (0, j)),
        out_shape=jax.ShapeDtypeStruct((nb, n), F32),
        compiler_params=_cparams(1),
        name="mod",
    )(c, w_ada, b_ada.reshape(1, n))


def _token_shift_mix(h, prev_row, next_row, mu):
    rows, cols = h.shape
    r8 = lax.broadcasted_iota(jnp.int32, (8, cols), 0)
    even = (lax.broadcasted_iota(jnp.int32, (rows, cols), 1) & 1) == 0
    prv = pltpu.roll(h, 1, axis=0)
    prv = jnp.concatenate([jnp.where(r8 == 0, prev_row, prv[0:8]), prv[8:]], axis=0)
    nxt = pltpu.roll(h, rows - 1, axis=0)
    nxt = jnp.concatenate([nxt[:rows - 8], jnp.where(r8 == 7, next_row, nxt[rows - 8:])], axis=0)
    return h + mu * (jnp.where(even, prv, nxt) - h)


def _inproj_kernel(seq, n_split, x_ref, xp_ref, xn_ref, sc_ref, sh_ref, g_ref, wrw_ref, wcv_ref, mu_ref,
                   orw_ref, ocv_ref, hn_ref):
    tm = x_ref.shape[0]
    tn = orw_ref.shape[1]
    t0 = (pl.program_id(0) * tm) % seq
    j = pl.program_id(1)

    def normed(x):
        ms = jnp.mean(x * x, axis=-1, keepdims=True)
        y = x * lax.rsqrt(ms + RMS_EPS) * g_ref[...]
        return y * (1.0 + sc_ref[0]) + sh_ref[0]

    def mix_store(h_all, cols):
        prev_row = jnp.where(t0 == 0, 0.0, h_all[tm + 7:tm + 8])
        next_row = jnp.where(t0 + tm == seq, 0.0, h_all[tm + 8:tm + 9])
        orw_ref[:, cols] = _token_shift_mix(h_all[0:tm], prev_row, next_row, mu_ref[:, cols]).astype(orw_ref.dtype)

    def conv_tile():
        ocv_ref[...] = _dot(hn_ref[0:tm, :], wcv_ref[...]).astype(ocv_ref.dtype)

    def rwkv_tile():
        for c0 in range(0, tn, MXU_COLS):
            cols = slice(c0, c0 + MXU_COLS)
            mix_store(_dot(hn_ref[...], wrw_ref[:, cols]), cols)

    @pl.when(j == 0)
    def _():
        hn_ref[tm:, :] = jnp.concatenate([normed(xp_ref[...]), normed(xn_ref[...])], axis=0).astype(BF16)
        rows = tm // n_split
        parts = []
        for q in range(n_split):
            sl = slice(q * rows, (q + 1) * rows)
            hn_ref[sl, :] = normed(x_ref[sl, :]).astype(BF16)
            sl = slice(q * rows, (q + 1) * rows if q + 1 < n_split else tm + 16)
            parts.append(_dot(hn_ref[sl, :], wrw_ref[...]))
        h_all = jnp.concatenate(parts, axis=0)
        for c0 in range(0, tn, MXU_COLS):
            mix_store(h_all[:, c0:c0 + MXU_COLS], slice(c0, c0 + MXU_COLS))

    @pl.when(j > 0)
    def _():
        rwkv_tile()
        conv_tile()


def _in_proj(x2, sc, sh, g1, w_rw, w_cv, mu_rw, seq):
    m, d = x2.shape
    tm, tn, n_split = MATMUL_ROWS, COL_TILE, NORM_SPLIT
    per_seq = seq // tm
    nblk8 = m // 8
    n_rw, n_conv = w_rw.shape[0], w_cv.shape[0]
    assert n_rw == n_conv + 1
    mod = lambda i, j: (i // per_seq, 0, 0)
    rw_tile = lambda j: (j + n_conv) % n_rw
    conv_tile = lambda j: jnp.maximum(j - 1, 0)
    return pl.pallas_call(
        functools.partial(_inproj_kernel, seq, n_split),
        grid=(m // tm, n_rw),
        in_specs=[pl.BlockSpec((tm, d), lambda i, j: (i, 0)),
                  pl.BlockSpec((8, d), lambda i, j: (jnp.maximum(i * (tm // 8) - 1, 0), 0)),
                  pl.BlockSpec((8, d), lambda i, j: (jnp.minimum((i + 1) * (tm // 8), nblk8 - 1), 0)),
                  pl.BlockSpec((1, 1, d), mod),
                  pl.BlockSpec((1, 1, d), mod),
                  pl.BlockSpec((1, d), lambda i, j: (0, 0)),
                  pl.BlockSpec((None, d, tn), lambda i, j: (rw_tile(j), 0, 0)),
                  pl.BlockSpec((None, d, tn), lambda i, j: (conv_tile(j), 0, 0)),
                  pl.BlockSpec((1, tn), lambda i, j: (0, rw_tile(j)))],
        out_specs=[pl.BlockSpec((None, tm, tn), lambda i, j: (rw_tile(j), i, 0)),
                   pl.BlockSpec((None, tm, tn), lambda i, j: (conv_tile(j), i, 0))],
        out_shape=[jax.ShapeDtypeStruct((n_rw, m, tn), BF16), jax.ShapeDtypeStruct((n_conv, m, tn), BF16)],
        scratch_shapes=[pltpu.VMEM((tm + 16, d), BF16)],
        compiler_params=_cparams(2),
        name="in_proj",
    )(x2, x2, x2, sc, sh, g1, w_rw, w_cv, mu_rw)


def _prep_kernel(h_ref, wfb_ref, wa_ref, wg_ref, tri_ref,
                 w0_ref, a0_ref, kk_ref, ka_ref, rk_ref,
                 k_out, kk_out, b_out, cumf_out, cumb_out, g_out, bonus_out):
    xs_sm = h_ref[RKV_COLS // COL_TILE].astype(F32)
    lora_fb = _dot(jnp.tanh(xs_sm[:, 0:LANES]).astype(BF16), wfb_ref[...])
    lora_a = _dot(xs_sm[:, LANES:2 * LANES].astype(BF16), wa_ref[...])
    g = _dot(_sigmoid(xs_sm[:, LANES:]).astype(BF16), wg_ref[...])

    half_scale = 0.5 * math.exp(-0.5)
    w0 = w0_ref[...]
    lw_f = -half_scale * (1.0 + jnp.tanh(0.5 * (w0[0:1] + lora_fb[:, :RWKV_WIDTH])))
    lw_b = -half_scale * (1.0 + jnp.tanh(0.5 * (w0[1:2] + lora_fb[:, RWKV_WIDTH:])))
    span = tri_ref.shape[1]
    for r0 in range(0, lw_f.shape[0], span):
        rows = slice(r0, r0 + span)
        cumf_out[rows, :] = _dot(tri_ref[0], lw_f[rows].astype(BF16))
        cumb_out[rows, :] = _dot(tri_ref[1], lw_b[rows].astype(BF16))
    g_out[...] = g.astype(g_out.dtype)
    a = _sigmoid(a0_ref[...] + lora_a)

    xk = _tile_cols(h_ref, RWKV_WIDTH, RWKV_WIDTH).astype(F32)
    kk_raw = xk * kk_ref[...]
    ss = _head_sum_bcast(kk_raw * kk_raw)
    kk = kk_raw * lax.rsqrt(jnp.maximum(ss, 1e-24))
    k = xk * (1.0 + (a - 1.0) * ka_ref[...])
    kk_out[...] = kk.astype(kk_out.dtype)
    b_out[...] = (kk * a).astype(b_out.dtype)
    k_out[...] = k.astype(k_out.dtype)
    r = _tile_cols(h_ref, 0, RWKV_WIDTH).astype(F32)
    rk_sum = _head_sum_bcast(r * k * rk_ref[...])
    v = _tile_cols(h_ref, 2 * RWKV_WIDTH, RWKV_WIDTH).astype(F32)
    bonus_out[...] = (rk_sum * v).astype(bonus_out.dtype)


def _prep(h, wfb, wa, wg, w0, a0, k_k, k_a, r_k):
    n_tiles, m, _ = h.shape
    tp = ELEMENTWISE_ROWS
    row = lambda i: (i, 0)
    const = lambda i: (0, 0)
    t = jnp.arange(MXU_COLS)
    same_chunk = (t[:, None] // CHUNK) == (t[None, :] // CHUNK)
    tri = jnp.stack([same_chunk & (t[None, :] <= t[:, None]),
                     same_chunk & (t[None, :] >= t[:, None])]).astype(BF16)
    out_dtypes = [BF16] * 3 + [F32, F32, BF16, BF16]
    return pl.pallas_call(
        _prep_kernel,
        grid=(m // tp,),
        in_specs=[pl.BlockSpec((n_tiles, tp, COL_TILE), lambda i: (0, i, 0)),
                  pl.BlockSpec(wfb.shape, const),
                  pl.BlockSpec(wa.shape, const),
                  pl.BlockSpec(wg.shape, const),
                  pl.BlockSpec(tri.shape, lambda i: (0, 0, 0)),
                  pl.BlockSpec((2, RWKV_WIDTH), const),
                  pl.BlockSpec((1, RWKV_WIDTH), const),
                  pl.BlockSpec((1, RWKV_WIDTH), const),
                  pl.BlockSpec((1, RWKV_WIDTH), const),
                  pl.BlockSpec((1, RWKV_WIDTH), const)],
        out_specs=[pl.BlockSpec((tp, RWKV_WIDTH), row)] * len(out_dtypes),
        out_shape=[jax.ShapeDtypeStruct((m, RWKV_WIDTH), dt) for dt in out_dtypes],
        compiler_params=_cparams(1),
        name="prep",
    )(h, wfb, wa, wg, tri, w0, a0, k_k, k_a, r_k)


def _block_diag(x, lane_lo):
    return jnp.concatenate([jnp.where(lane_lo, x, 0.0).astype(BF16),
                            jnp.where(lane_lo, 0.0, x).astype(BF16)], axis=0)


def _chunk_parallel(chains, common, out):
    eye2, lane_lo, _ = common
    bd = lambda x: _block_diag(x, lane_lo)
    n = len(chains)
    row = lax.broadcasted_iota(jnp.int32, (CHUNK, LANES), 0)
    at, rt, kt, bt, g_all = [], [], [], [], []
    for (r, k, v, kk, b, cum, _, bwd) in chains:
        gq = jnp.exp(cum)
        gk = jnp.exp(-cum)
        g_all.append(gq[0:1] if bwd else gq[CHUNK - 1:CHUNK])
        if bwd:
            excl = jnp.where(row == CHUNK - 1, 0.0, pltpu.roll(cum, CHUNK - 1, axis=0))
        else:
            excl = jnp.where(row == 0, 0.0, pltpu.roll(cum, 1, axis=0))
        at.append(-(kk * jnp.exp(excl)))
        rt.append(r * gq)
        kt.append(k * gk)
        bt.append(b * gk)
    yield

    a_ak, a_ab, a_rk, a_rb = [], [], [], []
    for c in range(n):
        tri_strict, tri_incl = chains[c][6]
        lhs = jnp.concatenate([at[c], rt[c]], axis=0).astype(BF16)
        rhs = jnp.concatenate([bd(kt[c]), bd(bt[c])], axis=0)
        big = _dot_nt(lhs, rhs)
        a_ak.append(jnp.where(tri_strict, big[0:CHUNK, 0:LANES], 0.0))
        a_ab.append(jnp.where(tri_strict, big[0:CHUNK, LANES:], 0.0))
        a_rk.append(jnp.where(tri_incl, big[CHUNK:, 0:LANES], 0.0))
        a_rb.append(jnp.where(tri_incl, big[CHUNK:, LANES:], 0.0))
    yield

    q = [eye2 + a for a in a_ab]
    p = [_dot(a.astype(BF16), bd(a)) for a in a_ab]
    xy = [_dot(jnp.concatenate([a_ak[c], a_rk[c]], axis=0).astype(BF16), bd(chains[c][2])) for c in range(n)]
    yield
    for _ in range(4):
        pq_pp = [_dot(p[c].astype(BF16), jnp.concatenate([bd(q[c]), bd(p[c])], axis=1)) for c in range(n)]
        q = [q[c] + pq_pp[c][:, 0:LANES] for c in range(n)]
        p = [pq_pp[c][:, LANES:] for c in range(n)]
        yield
    q = [q[c] + _dot(p[c].astype(BF16), bd(q[c])) for c in range(n)]
    yield
    wu = [_dot(q[c].astype(BF16), jnp.concatenate([bd(at[c]), bd(xy[c][0:CHUNK])], axis=1)) for c in range(n)]
    out.update(rt=rt, kt=kt, bt=bt, g_all=g_all, a_rb=a_rb, xy=xy, wu=wu, v=[ch[2] for ch in chains])


def _chunk_sequential(par, states, ys, common):
    _, lane_lo, bd_mask = common
    bd = lambda x: _block_diag(x, lane_lo)
    n = len(states)
    wu, rt, xy = par["wu"], par["rt"], par["xy"]
    s1 = [_dot_nt(jnp.concatenate([wu[c][:, 0:LANES], rt[c]], axis=0).astype(BF16), states[c].astype(BF16))
          for c in range(n)]
    u = [wu[c][:, LANES:] + s1[c][0:CHUNK] for c in range(n)]
    yield
    ys.extend(s1[c][CHUNK:] + xy[c][CHUNK:] + _dot(par["a_rb"][c].astype(BF16), bd(u[c])) for c in range(n))
    yield
    for c in range(n):
        vu_t = jnp.concatenate([par["v"][c], u[c]], axis=0).T.astype(BF16)
        kb = jnp.concatenate([par["kt"][c], par["bt"][c]], axis=0).astype(BF16)
        states[c] = jnp.where(bd_mask, (states[c] + _dot(vu_t, kb)) * par["g_all"][c], 0.0)
    yield


def _interleave(*gens):
    gens = list(gens)
    while gens:
        for g in list(gens):
            try:
                next(g)
            except StopIteration:
                gens.remove(g)


def _chain_gens(*gens):
    for g in gens:
        yield from g


def _scan_kernel(n_chunks,
                 rf_ref, kf_ref, vf_ref, kkf_ref, bf_ref, cumf_ref,
                 rb_ref, kb_ref, vb_ref, kkb_ref, bb_ref, cumb_ref,
                 yf_ref, yb_ref, state_ref):
    @pl.when(pl.program_id(1) == 0)
    def _():
        state_ref[...] = jnp.zeros_like(state_ref)

    n_pairs = kf_ref.shape[1] // LANES
    ri = lax.broadcasted_iota(jnp.int32, (CHUNK, LANES), 0)
    ci = lax.broadcasted_iota(jnp.int32, (CHUNK, LANES), 1) & (CHUNK - 1)
    lane_lo = lax.broadcasted_iota(jnp.int32, (CHUNK, LANES), 1) < CHUNK
    eye2 = jnp.where(ri == ci, 1.0, 0.0)
    br = lax.broadcasted_iota(jnp.int32, (LANES, LANES), 0) >> 6
    bc = lax.broadcasted_iota(jnp.int32, (LANES, LANES), 1) >> 6
    common = (eye2, lane_lo, br == bc)
    tri_f = (ci < ri, ci <= ri)
    tri_b = (ci > ri, ci >= ri)

    def rows(step):
        return (slice(step * CHUNK, (step + 1) * CHUNK),
                slice((n_chunks - 1 - step) * CHUNK, (n_chunks - step) * CHUNK))

    def parallel(step, out):
        rows_f, rows_b = rows(step)
        chains = []
        for p in range(n_pairs):
            cols = slice(p * LANES, (p + 1) * LANES)

            def tok(ref, rws):
                if len(ref.shape) == 3:
                    per_tile = COL_TILE // LANES
                    lanes = slice((p % per_tile) * LANES, (p % per_tile + 1) * LANES)
                    return ref[p // per_tile, rws, lanes].astype(F32)
                return ref[rws, cols].astype(F32)

            tok_f = [tok(ref, rows_f) for ref in (rf_ref, kf_ref, vf_ref, kkf_ref, bf_ref)]
            tok_b = [tok(ref, rows_b) for ref in (rb_ref, kb_ref, vb_ref, kkb_ref, bb_ref)]
            chains.append((*tok_f, cumf_ref[rows_f, cols], tri_f, False))
            chains.append((*tok_b, cumb_ref[rows_b, cols], tri_b, True))
        yield from _chunk_parallel(chains, common, out)

    def sequential(step, par):
        rows_f, rows_b = rows(step)
        ys = []
        yield from _chunk_sequential(par, states, ys, common)
        for p in range(n_pairs):
            cols = slice(p * LANES, (p + 1) * LANES)
            yf_ref[rows_f, cols] = ys[2 * p].astype(yf_ref.dtype)
            yb_ref[rows_b, cols] = ys[2 * p + 1].astype(yb_ref.dtype)

    states = [state_ref[d, p] for p in range(n_pairs) for d in range(2)]
    pending = None
    for step in range(0, n_chunks, 2):
        par = [{}, {}]
        gens = [parallel(step, par[0]), parallel(step + 1, par[1])]
        if pending is not None:
            gens.append(_chain_gens(sequential(step - 2, pending[0]), sequential(step - 1, pending[1])))
        _interleave(*gens)
        pending = par
    _interleave(_chain_gens(sequential(n_chunks - 2, pending[0]), sequential(n_chunks - 1, pending[1])))
    for p in range(n_pairs):
        for d in range(2):
            state_ref[d, p] = states[2 * p + d]


def _scan(h, k, kk, b, cumf, cumb, seq):
    m = k.shape[0]
    n_chunks = SCAN_CHUNKS
    tb = n_chunks * CHUNK
    nb = seq // tb
    group = RWKV_WIDTH // COL_TILE
    fwd = lambda s, i: (s * nb + i, 0)
    bwd = lambda s, i: (s * nb + nb - 1 - i, 0)
    blk = lambda im: pl.BlockSpec((tb, RWKV_WIDTH), im)
    hblk = lambda im, g: pl.BlockSpec((group, tb, COL_TILE), lambda s, i: (g, im(s, i)[0], 0))
    out = jax.ShapeDtypeStruct((m, RWKV_WIDTH), BF16)
    specs = lambda im: [hblk(im, 0), blk(im), hblk(im, 2), blk(im), blk(im), blk(im)]
    return pl.pallas_call(
        functools.partial(_scan_kernel, n_chunks),
        grid=(m // seq, nb),
        in_specs=specs(fwd) + specs(bwd),
        out_specs=[blk(fwd), blk(bwd)],
        out_shape=[out, out],
        scratch_shapes=[pltpu.VMEM((2, RWKV_WIDTH // LANES, LANES, LANES), F32)],
        compiler_params=_cparams(2),
        name="scan",
    )(h, k, h, kk, b, cumf, h, k, h, kk, b, cumb)


def _post_kernel(seq,
                 yf_ref, yb_ref, g_ref, bonus_ref, cv_ref, cvp_ref, cvn_ref,
                 lng_ref, lnb_ref, cw_ref, x_ref, gt_ref, w_ref, o_ref, mix_ref):
    tm = yf_ref.shape[0]
    t0 = (pl.program_id(0) * tm) % seq
    first = t0 == 0
    last = t0 + tm == seq
    y = yf_ref[...].astype(F32) + yb_ref[...].astype(F32)
    mean = _head_sum_bcast(y) * (1.0 / HEAD)
    d = y - mean
    var = _head_sum_bcast(d * d) * (1.0 / HEAD)
    yn = d * lax.rsqrt(var + LNX_EPS) * lng_ref[...] + lnb_ref[...]
    mix_ref[:, 0:RWKV_WIDTH] = ((yn + bonus_ref[...].astype(F32)) * g_ref[...].astype(F32)).astype(BF16)

    gate_z = lambda ref, rws=slice(None): (_tile_cols(ref, CONV_WIDTH, CONV_WIDTH, rws).astype(F32)
                                           * _tile_cols(ref, 2 * CONV_WIDTH, CONV_WIDTH, rws).astype(F32))
    z = gate_z(cv_ref)
    zp = jnp.where(first, 0.0, gate_z(cvp_ref)[HALO - 1:HALO])
    zn = jnp.where(last, 0.0, gate_z(cvn_ref)[0:1])
    ridx = lax.broadcasted_iota(jnp.int32, z.shape, 0)
    z_prev = jnp.where(ridx == 0, zp, pltpu.roll(z, 1, axis=0))
    z_next = jnp.where(ridx == tm - 1, zn, pltpu.roll(z, tm - 1, axis=0))
    cw = cw_ref[...]
    conv = z_prev * cw[0:1] + z * cw[1:2] + z_next * cw[2:3]
    mix_ref[:, RWKV_WIDTH:] = (_tile_cols(cv_ref, 0, CONV_WIDTH).astype(F32) * conv).astype(BF16)

    gt = gt_ref[0]
    for c0 in range(0, o_ref.shape[1], POST_OUT_COLS):
        cols = slice(c0, c0 + POST_OUT_COLS)
        o_ref[:, cols] = x_ref[:, cols] + gt[:, cols] * _dot(mix_ref[...], w_ref[:, cols])


def _post(yf, yb, g, bonus, h, lnx_g, lnx_b, conv_w, x2, gt1, w_out, seq):
    m, d = x2.shape
    tm = ELEMENTWISE_ROWS
    per_seq = seq // tm
    n_halo = m // HALO
    n_cv = h.shape[0]
    row = lambda i: (i, 0)
    const = lambda i: (0, 0)
    return pl.pallas_call(
        functools.partial(_post_kernel, seq),
        grid=(m // tm,),
        in_specs=[pl.BlockSpec((tm, RWKV_WIDTH), row)] * 4 + [
            pl.BlockSpec((n_cv, tm, COL_TILE), lambda i: (0, i, 0)),
            pl.BlockSpec((n_cv, HALO, COL_TILE), lambda i: (0, jnp.maximum(i * (tm // HALO) - 1, 0), 0)),
            pl.BlockSpec((n_cv, HALO, COL_TILE),
                         lambda i: (0, jnp.minimum((i + 1) * (tm // HALO), n_halo - 1), 0)),
            pl.BlockSpec((1, RWKV_WIDTH), const),
            pl.BlockSpec((1, RWKV_WIDTH), const),
            pl.BlockSpec((3, CONV_WIDTH), const),
            pl.BlockSpec((tm, d), row),
            pl.BlockSpec((1, 1, d), lambda i: (i // per_seq, 0, 0)),
            pl.BlockSpec((d, d), const, pipeline_mode=pl.Buffered(1))],
        out_specs=pl.BlockSpec((tm, d), row),
        out_shape=jax.ShapeDtypeStruct((m, d), F32),
        scratch_shapes=[pltpu.VMEM((tm, d), BF16)],
        compiler_params=_cparams(1),
        name="post",
    )(yf, yb, g, bonus, h, h, h, lnx_g, lnx_b, conv_w, x2, gt1, w_out)


def _ffn_kernel(n_split, x_ref, sc_ref, sh_ref, gt_ref, g2_ref, gf_ref, wg_ref, wu_ref, wd_ref, o_ref, hn_ref):
    j = pl.program_id(1)
    last = pl.num_programs(1) - 1
    tm, n_out = o_ref.shape
    rows = tm // n_split

    def swiglu_step(sl, init):
        hn = hn_ref[sl, :]
        gate = _dot(hn, wg_ref[...])
        up = _dot(hn, wu_ref[...])
        act = (gate * jax.nn.sigmoid(gate) * up).astype(BF16)
        for c0 in range(0, n_out, FFN_DOWN_COLS):
            cols = slice(c0, c0 + FFN_DOWN_COLS)
            part = _dot(act, wd_ref[:, cols])
            if init:
                o_ref[sl, cols] = part
            else:
                o_ref[sl, cols] += part

    @pl.when(j == 0)
    def _():
        for q in range(n_split):
            sl = slice(q * rows, (q + 1) * rows)
            x = x_ref[sl, :]
            ms = jnp.mean(x * x, axis=-1, keepdims=True)
            y = x * lax.rsqrt(ms + RMS_EPS) * g2_ref[...]
            hn_ref[sl, :] = (y * (1.0 + sc_ref[0]) + sh_ref[0]).astype(BF16)
            swiglu_step(sl, True)

    @pl.when((j > 0) & (j < last))
    def _():
        swiglu_step(slice(0, tm), False)

    @pl.when(j == last)
    def _():
        for q in range(n_split):
            sl = slice(q * rows, (q + 1) * rows)
            swiglu_step(sl, False)
            x = x_ref[sl, :] + gt_ref[0] * o_ref[sl, :]
            ms = jnp.mean(x * x, axis=-1, keepdims=True)
            o_ref[sl, :] = x * lax.rsqrt(ms + RMS_EPS) * gf_ref[...]


def _ffn(x1, sc, sh, gt, g2, gf, wg, wu, wd, seq):
    m, d = x1.shape
    tm, tf, n_split = MATMUL_ROWS, FFN_TILE, NORM_SPLIT
    n_ff = wg.shape[1] // tf
    per_seq = seq // tm
    row = lambda i, j: (i, 0)
    mod = lambda i, j: (i // per_seq, 0, 0)
    const = lambda i, j: (0, 0)
    return pl.pallas_call(
        functools.partial(_ffn_kernel, n_split),
        grid=(m // tm, n_ff),
        in_specs=[pl.BlockSpec((tm, d), row),
                  pl.BlockSpec((1, 1, d), mod),
                  pl.BlockSpec((1, 1, d), mod),
                  pl.BlockSpec((1, 1, d), mod),
                  pl.BlockSpec((1, d), const),
                  pl.BlockSpec((1, d), const),
                  pl.BlockSpec((d, tf), lambda i, j: (0, j)),
                  pl.BlockSpec((d, tf), lambda i, j: (0, j)),
                  pl.BlockSpec((tf, d), lambda i, j: (j, 0))],
        out_specs=pl.BlockSpec((tm, d), row),
        out_shape=jax.ShapeDtypeStruct((m, d), F32),
        scratch_shapes=[pltpu.VMEM((tm, d), BF16)],
        compiler_params=_cparams(2),
        name="ffn",
    )(x1, sc, sh, gt, g2, gf, wg, wu, wd)


def _layer_weights(w_in, mu_shift, w2_decay, a2, g2_gate):
    d = w_in.shape[0]
    n_rw = RKV_COLS + SMALL_COLS
    pad = RW_COLS - n_rw
    tile_major = lambda w: w.reshape(d, -1, COL_TILE).transpose(1, 0, 2)
    w_rw = tile_major(jnp.pad(w_in[:, :n_rw].astype(BF16), ((0, 0), (0, pad))))
    w_cv = tile_major(w_in[:, n_rw:].astype(BF16))
    mu_rw = jnp.pad(mu_shift, (0, pad)).reshape(1, RW_COLS)
    zeros = lambda rows, cols: jnp.zeros((rows, cols), F32)
    wfb = jnp.concatenate([
        jnp.concatenate([w2_decay[0], zeros(DECAY_RANK, RWKV_WIDTH)], axis=1),
        jnp.concatenate([zeros(DECAY_RANK, RWKV_WIDTH), w2_decay[1]], axis=1)], axis=0).astype(BF16)
    wa = jnp.concatenate([a2, zeros(LANES - AAA_RANK, RWKV_WIDTH)], axis=0).astype(BF16)
    g_rows = SMALL_PAD - LANES
    wg = jnp.concatenate([zeros(AAA_RANK, RWKV_WIDTH), g2_gate,
                          zeros(g_rows - AAA_RANK - GATE_RANK, RWKV_WIDTH)], axis=0).astype(BF16)
    return w_rw, w_cv, mu_rw, wfb, wa, wg


def _group(x, mod, lw):
    nb, seq, d = x.shape
    x2 = x.reshape(nb * seq, d)
    sh1, sc1, gt1, sh2, sc2, gt2 = [mod[:, q * d:(q + 1) * d].reshape(nb, 1, d) for q in range(6)]
    h, h_cv = _in_proj(x2, sc1, sh1, lw["norm1_g"], lw["w_rw"], lw["w_cv"], lw["mu_rw"], seq)
    k, kk, b, cumf, cumb, g, bonus = _prep(
        h, lw["wfb"], lw["wa"], lw["wg"], lw["w0"], lw["a0"], lw["k_k"], lw["k_a"], lw["r_k"])
    yf, yb = _scan(h, k, kk, b, cumf, cumb, seq)
    x1 = _post(yf, yb, g, bonus, h_cv, lw["lnx_g"], lw["lnx_b"], lw["conv_w"], x2, gt1, lw["w_out"], seq)
    out = _ffn(x1, sc2, sh2, gt2, lw["norm2_g"], lw["norm_f_g"], lw["wg_ffn"], lw["wu_ffn"], lw["wd_ffn"], seq)
    return out.reshape(nb, seq, d)


def kernel(x_prompt, x_sample, c_prompt, c_sample, w_ada, b_ada, norm1_g, w_in, mu_shift, w0_decay, w2_decay, a0, a2, g2_gate, k_k, k_a, r_k, lnx_g, lnx_b, conv_w, w_out, norm2_g, w_ffn_gate, w_ffn_up, w_ffn_down, norm_f_g):
    assert w_ada.shape[0] == 1, "single-layer block"
    l = 0
    row = lambda t: t.reshape(1, -1)
    w_rw, w_cv, mu_rw, wfb, wa, wg = _layer_weights(w_in[l], mu_shift[l], w2_decay[l], a2[l], g2_gate[l])
    lw = dict(w_rw=w_rw, w_cv=w_cv, mu_rw=mu_rw, wfb=wfb, wa=wa, wg=wg,
              norm1_g=row(norm1_g[l]), w0=w0_decay[l], a0=row(a0[l]), k_k=row(k_k[l]), k_a=row(k_a[l]),
              r_k=row(r_k[l]), lnx_g=row(lnx_g[l]), lnx_b=row(lnx_b[l]), conv_w=conv_w[l],
              w_out=w_out[l].astype(BF16), norm2_g=row(norm2_g[l]), norm_f_g=row(norm_f_g),
              wg_ffn=w_ffn_gate[l].astype(BF16), wu_ffn=w_ffn_up[l].astype(BF16),
              wd_ffn=w_ffn_down[l].astype(BF16))
    n_prompt = c_prompt.shape[0]
    mod = _mod(jnp.concatenate([c_prompt, c_sample], axis=0), w_ada[l], b_ada[l])
    y_prompt = _group(x_prompt, mod[:n_prompt], lw)
    y_sample = _group(x_sample, mod[n_prompt:], lw)
    return (y_prompt, y_sample)
```

```python
import functools
import math

import jax
import jax.numpy as jnp
from jax import lax
from jax.experimental import pallas as pl
from jax.experimental.pallas import tpu as pltpu

F32 = jnp.float32
BF16 = jnp.bfloat16

D_MODEL = 2048
RWKV_WIDTH = 1024
CONV_WIDTH = 1024
HEAD = 64
DECAY_RANK = 64
AAA_RANK = 64
GATE_RANK = 160
SMALL_COLS = 2 * DECAY_RANK + AAA_RANK + GATE_RANK
SMALL_PAD = 512
RKV_COLS = 3 * RWKV_WIDTH
CONV_COLS = 3 * CONV_WIDTH
RW_COLS = RKV_COLS + SMALL_PAD
D_FF = 5632
RMS_EPS = 1e-6
LNX_EPS = 64e-5

CHUNK = 64
LANES = 128
MXU_COLS = 256
HALO = 16
COL_TILE = 512
FFN_TILE = 512
MATMUL_ROWS = 1024
NORM_SPLIT = 4
ELEMENTWISE_ROWS = 512
SCAN_CHUNKS = 4
MOD_COLS = 1024
VMEM_LIMIT = 56 * 1024 * 1024
FFN_DOWN_COLS = 512
POST_OUT_COLS = 512


def _cparams(n_axes):
    return pltpu.CompilerParams(dimension_semantics=("arbitrary",) * n_axes,
                                vmem_limit_bytes=VMEM_LIMIT)


def _dot(a, b):
    return jnp.dot(a, b, preferred_element_type=F32)


def _dot_nt(a, b):
    return lax.dot_general(a, b, (((1,), (1,)), ((), ())), preferred_element_type=F32)


def _tile_cols(ref, c0, width, rows=slice(None)):
    tiles = range(c0 // COL_TILE, (c0 + width) // COL_TILE)
    return jnp.concatenate([ref[t, rows, :] for t in tiles], axis=1)


def _sigmoid(x):
    return 0.5 * jnp.tanh(0.5 * x) + 0.5


def _head_sum_bcast(x):
    n = MXU_COLS
    row = lax.broadcasted_iota(jnp.int32, (n, n), 0) >> 6
    col = lax.broadcasted_iota(jnp.int32, (n, n), 1) >> 6
    ones_bd = jnp.where(row == col, 1.0, 0.0).astype(BF16)
    outs = [_dot(x[:, q * n:(q + 1) * n].astype(BF16), ones_bd) for q in range(x.shape[1] // n)]
    return jnp.concatenate(outs, axis=1)


def _mod_kernel(c_ref, w_ref, b_ref, o_ref):
    c = c_ref[...]
    s = c * jax.nn.sigmoid(c)
    o_ref[...] = _dot(s.astype(BF16), w_ref[...].astype(BF16)) + b_ref[...]


def _mod(c, w_ada, b_ada):
    nb, d = c.shape
    n = w_ada.shape[1]
    tn = MOD_COLS
    return pl.pallas_call(
        _mod_kernel,
        grid=(n // tn,),
        in_specs=[pl.BlockSpec((nb, d), lambda j: (0, 0)),
                  pl.BlockSpec((d, tn), lambda j: (0, j)),
                  pl.BlockSpec((1, tn), lambda j: (0, j))],
        out_specs=pl.BlockSpec((nb, tn), lambda j: (0, j)),
        out_shape=jax.ShapeDtypeStruct((nb, n), F32),
        compiler_params=_cparams(1),
        name="mod",
    )(c, w_ada, b_ada.reshape(1, n))


def _token_shift_mix(h, prev_row, next_row, mu):
    rows, cols = h.shape
    r8 = lax.broadcasted_iota(jnp.int32, (8, cols), 0)
    even = (lax.broadcasted_iota(jnp.int32, (rows, cols), 1) & 1) == 0
    prv = pltpu.roll(h, 1, axis=0)
    prv = jnp.concatenate([jnp.where(r8 == 0, prev_row, prv[0:8]), prv[8:]], axis=0)
    nxt = pltpu.roll(h, rows - 1, axis=0)
    nxt = jnp.concatenate([nxt[:rows - 8], jnp.where(r8 == 7, next_row, nxt[rows - 8:])], axis=0)
    return h + mu * (jnp.where(even, prv, nxt) - h)


def _inproj_kernel(seq, n_split, x_ref, xp_ref, xn_ref, sc_ref, sh_ref, g_ref, wrw_ref, wcv_ref, mu_ref,
                   orw_ref, ocv_ref, hn_ref):
    tm = x_ref.shape[0]
    tn = orw_ref.shape[1]
    t0 = (pl.program_id(0) * tm) % seq
    j = pl.program_id(1)

    def normed(x):
        ms = jnp.mean(x * x, axis=-1, keepdims=True)
        y = x * lax.rsqrt(ms + RMS_EPS) * g_ref[...]
        return y * (1.0 + sc_ref[0]) + sh_ref[0]

    def mix_store(h_all, cols):
        prev_row = jnp.where(t0 == 0, 0.0, h_all[tm + 7:tm + 8])
        next_row = jnp.where(t0 + tm == seq, 0.0, h_all[tm + 8:tm + 9])
        orw_ref[:, cols] = _token_shift_mix(h_all[0:tm], prev_row, next_row, mu_ref[:, cols]).astype(orw_ref.dtype)

    def conv_tile():
        ocv_ref[...] = _dot(hn_ref[0:tm, :], wcv_ref[...]).astype(ocv_ref.dtype)

    def rwkv_tile():
        for c0 in range(0, tn, MXU_COLS):
            cols = slice(c0, c0 + MXU_COLS)
            mix_store(_dot(hn_ref[...], wrw_ref[:, cols]), cols)

    @pl.when(j == 0)
    def _():
        hn_ref[tm:, :] = jnp.concatenate([normed(xp_ref[...]), normed(xn_ref[...])], axis=0).astype(BF16)
        rows = tm // n_split
        parts = []
        for q in range(n_split):
            sl = slice(q * rows, (q + 1) * rows)
            hn_ref[sl, :] = normed(x_ref[sl, :]).astype(BF16)
            sl = slice(q * rows, (q + 1) * rows if q + 1 < n_split else tm + 16)
            parts.append(_dot(hn_ref[sl, :], wrw_ref[...]))
        h_all = jnp.concatenate(parts, axis=0)
        for c0 in range(0, tn, MXU_COLS):
            mix_store(h_all[:, c0:c0 + MXU_COLS], slice(c0, c0 + MXU_COLS))

    @pl.when(j > 0)
    def _():
        rwkv_tile()
        conv_tile()


def _in_proj(x2, sc, sh, g1, w_rw, w_cv, mu_rw, seq):
    m, d = x2.shape
    tm, tn, n_split = MATMUL_ROWS, COL_TILE, NORM_SPLIT
    per_seq = seq // tm
    nblk8 = m // 8
    n_rw, n_conv = w_rw.shape[0], w_cv.shape[0]
    assert n_rw == n_conv + 1
    mod = lambda i, j: (i // per_seq, 0, 0)
    rw_tile = lambda j: (j + n_conv) % n_rw
    conv_tile = lambda j: jnp.maximum(j - 1, 0)
    return pl.pallas_call(
        functools.partial(_inproj_kernel, seq, n_split),
        grid=(m // tm, n_rw),
        in_specs=[pl.BlockSpec((tm, d), lambda i, j: (i, 0)),
                  pl.BlockSpec((8, d), lambda i, j: (jnp.maximum(i * (tm // 8) - 1, 0), 0)),
                  pl.BlockSpec((8, d), lambda i, j: (jnp.minimum((i + 1) * (tm // 8), nblk8 - 1), 0)),
                  pl.BlockSpec((1, 1, d), mod),
                  pl.BlockSpec((1, 1, d), mod),
                  pl.BlockSpec((1, d), lambda i, j: (0, 0)),
                  pl.BlockSpec((None, d, tn), lambda i, j: (rw_tile(j), 0, 0)),
                  pl.BlockSpec((None, d, tn), lambda i, j: (conv_tile(j), 0, 0)),
                  pl.BlockSpec((1, tn), lambda i, j: (0, rw_tile(j)))],
        out_specs=[pl.BlockSpec((None, tm, tn), lambda i, j: (rw_tile(j), i, 0)),
                   pl.BlockSpec((None, tm, tn), lambda i, j: (conv_tile(j), i, 0))],
        out_shape=[jax.ShapeDtypeStruct((n_rw, m, tn), BF16), jax.ShapeDtypeStruct((n_conv, m, tn), BF16)],
        scratch_shapes=[pltpu.VMEM((tm + 16, d), BF16)],
        compiler_params=_cparams(2),
        name="in_proj",
    )(x2, x2, x2, sc, sh, g1, w_rw, w_cv, mu_rw)


def _prep_kernel(h_ref, wfb_ref, wa_ref, wg_ref, tri_ref,
                 w0_ref, a0_ref, kk_ref, ka_ref, rk_ref,
                 k_out, kk_out, b_out, cumf_out, cumb_out, g_out, bonus_out):
    xs_sm = h_ref[RKV_COLS // COL_TILE].astype(F32)
    lora_fb = _dot(jnp.tanh(xs_sm[:, 0:LANES]).astype(BF16), wfb_ref[...])
    lora_a = _dot(xs_sm[:, LANES:2 * LANES].astype(BF16), wa_ref[...])
    g = _dot(_sigmoid(xs_sm[:, LANES:]).astype(BF16), wg_ref[...])

    half_scale = 0.5 * math.exp(-0.5)
    w0 = w0_ref[...]
    lw_f = -half_scale * (1.0 + jnp.tanh(0.5 * (w0[0:1] + lora_fb[:, :RWKV_WIDTH])))
    lw_b = -half_scale * (1.0 + jnp.tanh(0.5 * (w0[1:2] + lora_fb[:, RWKV_WIDTH:])))
    span = tri_ref.shape[1]
    for r0 in range(0, lw_f.shape[0], span):
        rows = slice(r0, r0 + span)
        cumf_out[rows, :] = _dot(tri_ref[0], lw_f[rows].astype(BF16))
        cumb_out[rows, :] = _dot(tri_ref[1], lw_b[rows].astype(BF16))
    g_out[...] = g.astype(g_out.dtype)
    a = _sigmoid(a0_ref[...] + lora_a)

    xk = _tile_cols(h_ref, RWKV_WIDTH, RWKV_WIDTH).astype(F32)
    kk_raw = xk * kk_ref[...]
    ss = _head_sum_bcast(kk_raw * kk_raw)
    kk = kk_raw * lax.rsqrt(jnp.maximum(ss, 1e-24))
    k = xk * (1.0 + (a - 1.0) * ka_ref[...])
    kk_out[...] = kk.astype(kk_out.dtype)
    b_out[...] = (kk * a).astype(b_out.dtype)
    k_out[...] = k.astype(k_out.dtype)
    r = _tile_cols(h_ref, 0, RWKV_WIDTH).astype(F32)
    rk_sum = _head_sum_bcast(r * k * rk_ref[...])
    v = _tile_cols(h_ref, 2 * RWKV_WIDTH, RWKV_WIDTH).astype(F32)
    bonus_out[...] = (rk_sum * v).astype(bonus_out.dtype)


def _prep(h, wfb, wa, wg, w0, a0, k_k, k_a, r_k):
    n_tiles, m, _ = h.shape
    tp = ELEMENTWISE_ROWS
    row = lambda i: (i, 0)
    const = lambda i: (0, 0)
    t = jnp.arange(MXU_COLS)
    same_chunk = (t[:, None] // CHUNK) == (t[None, :] // CHUNK)
    tri = jnp.stack([same_chunk & (t[None, :] <= t[:, None]),
                     same_chunk & (t[None, :] >= t[:, None])]).astype(BF16)
    out_dtypes = [BF16] * 3 + [F32, F32, BF16, BF16]
    return pl.pallas_call(
        _prep_kernel,
        grid=(m // tp,),
        in_specs=[pl.BlockSpec((n_tiles, tp, COL_TILE), lambda i: (0, i, 0)),
                  pl.BlockSpec(wfb.shape, const),
                  pl.BlockSpec(wa.shape, const),
                  pl.BlockSpec(wg.shape, const),
                  pl.BlockSpec(tri.shape, lambda i: (0, 0, 0)),
                  pl.BlockSpec((2, RWKV_WIDTH), const),
                  pl.BlockSpec((1, RWKV_WIDTH), const),
                  pl.BlockSpec((1, RWKV_WIDTH), const),
                  pl.BlockSpec((1, RWKV_WIDTH), const),
                  pl.BlockSpec((1, RWKV_WIDTH), const)],
        out_specs=[pl.BlockSpec((tp, RWKV_WIDTH), row)] * len(out_dtypes),
        out_shape=[jax.ShapeDtypeStruct((m, RWKV_WIDTH), dt) for dt in out_dtypes],
        compiler_params=_cparams(1),
        name="prep",
    )(h, wfb, wa, wg, tri, w0, a0, k_k, k_a, r_k)


def _block_diag(x, lane_lo):
    return jnp.concatenate([jnp.where(lane_lo, x, 0.0).astype(BF16),
                            jnp.where(lane_lo, 0.0, x).astype(BF16)], axis=0)


def _chunk_parallel(chains, common, out):
    eye2, lane_lo, _ = common
    bd = lambda x: _block_diag(x, lane_lo)
    n = len(chains)
    row = lax.broadcasted_iota(jnp.int32, (CHUNK, LANES), 0)
    at, rt, kt, bt, g_all = [], [], [], [], []
    for (r, k, v, kk, b, cum, _, bwd) in chains:
        gq = jnp.exp(cum)
        gk = jnp.exp(-cum)
        g_all.append(gq[0:1] if bwd else gq[CHUNK - 1:CHUNK])
        if bwd:
            excl = jnp.where(row == CHUNK - 1, 0.0, pltpu.roll(cum, CHUNK - 1, axis=0))
        else:
            excl = jnp.where(row == 0, 0.0, pltpu.roll(cum, 1, axis=0))
        at.append(-(kk * jnp.exp(excl)))
        rt.append(r * gq)
        kt.append(k * gk)
        bt.append(b * gk)
    yield

    a_ak, a_ab, a_rk, a_rb = [], [], [], []
    for c in range(n):
        tri_strict, tri_incl = chains[c][6]
        lhs = jnp.concatenate([at[c], rt[c]], axis=0).astype(BF16)
        rhs = jnp.concatenate([bd(kt[c]), bd(bt[c])], axis=0)
        big = _dot_nt(lhs, rhs)
        a_ak.append(jnp.where(tri_strict, big[0:CHUNK, 0:LANES], 0.0))
        a_ab.append(jnp.where(tri_strict, big[0:CHUNK, LANES:], 0.0))
        a_rk.append(jnp.where(tri_incl, big[CHUNK:, 0:LANES], 0.0))
        a_rb.append(jnp.where(tri_incl, big[CHUNK:, LANES:], 0.0))
    yield

    q = [eye2 + a for a in a_ab]
    p = [_dot(a.astype(BF16), bd(a)) for a in a_ab]
    xy = [_dot(jnp.concatenate([a_ak[c], a_rk[c]], axis=0).astype(BF16), bd(chains[c][2])) for c in range(n)]
    yield
    for _ in range(4):
        pq_pp = [_dot(p[c].astype(BF16), jnp.concatenate([bd(q[c]), bd(p[c])], axis=1)) for c in range(n)]
        q = [q[c] + pq_pp[c][:, 0:LANES] for c in range(n)]
        p = [pq_pp[c][:, LANES:] for c in range(n)]
        yield
    q = [q[c] + _dot(p[c].astype(BF16), bd(q[c])) for c in range(n)]
    yield
    wu = [_dot(q[c].astype(BF16), jnp.concatenate([bd(at[c]), bd(xy[c][0:CHUNK])], axis=1)) for c in range(n)]
    out.update(rt=rt, kt=kt, bt=bt, g_all=g_all, a_rb=a_rb, xy=xy, wu=wu, v=[ch[2] for ch in chains])


def _chunk_sequential(par, states, ys, common):
    _, lane_lo, bd_mask = common
    bd = lambda x: _block_diag(x, lane_lo)
    n = len(states)
    wu, rt, xy = par["wu"], par["rt"], par["xy"]
    s1 = [_dot_nt(jnp.concatenate([wu[c][:, 0:LANES], rt[c]], axis=0).astype(BF16), states[c].astype(BF16))
          for c in range(n)]
    u = [wu[c][:, LANES:] + s1[c][0:CHUNK] for c in range(n)]
    yield
    ys.extend(s1[c][CHUNK:] + xy[c][CHUNK:] + _dot(par["a_rb"][c].astype(BF16), bd(u[c])) for c in range(n))
    yield
    for c in range(n):
        vu_t = jnp.concatenate([par["v"][c], u[c]], axis=0).T.astype(BF16)
        kb = jnp.concatenate([par["kt"][c], par["bt"][c]], axis=0).astype(BF16)
        states[c] = jnp.where(bd_mask, (states[c] + _dot(vu_t, kb)) * par["g_all"][c], 0.0)
    yield


def _interleave(*gens):
    gens = list(gens)
    while gens:
        for g in list(gens):
            try:
                next(g)
            except StopIteration:
                gens.remove(g)


def _chain_gens(*gens):
    for g in gens:
        yield from g


def _scan_kernel(n_chunks,
                 rf_ref, kf_ref, vf_ref, kkf_ref, bf_ref, cumf_ref,
                 rb_ref, kb_ref, vb_ref, kkb_ref, bb_ref, cumb_ref,
                 yf_ref, yb_ref, state_ref):
    @pl.when(pl.program_id(1) == 0)
    def _():
        state_ref[...] = jnp.zeros_like(state_ref)

    n_pairs = kf_ref.shape[1] // LANES
    ri = lax.broadcasted_iota(jnp.int32, (CHUNK, LANES), 0)
    ci = lax.broadcasted_iota(jnp.int32, (CHUNK, LANES), 1) & (CHUNK - 1)
    lane_lo = lax.broadcasted_iota(jnp.int32, (CHUNK, LANES), 1) < CHUNK
    eye2 = jnp.where(ri == ci, 1.0, 0.0)
    br = lax.broadcasted_iota(jnp.int32, (LANES, LANES), 0) >> 6
    bc = lax.broadcasted_iota(jnp.int32, (LANES, LANES), 1) >> 6
    common = (eye2, lane_lo, br == bc)
    tri_f = (ci < ri, ci <= ri)
    tri_b = (ci > ri, ci >= ri)

    def rows(step):
        return (slice(step * CHUNK, (step + 1) * CHUNK),
                slice((n_chunks - 1 - step) * CHUNK, (n_chunks - step) * CHUNK))

    def parallel(step, out):
        rows_f, rows_b = rows(step)
        chains = []
        for p in range(n_pairs):
            cols = slice(p * LANES, (p + 1) * LANES)

            def tok(ref, rws):
                if len(ref.shape) == 3:
                    per_tile = COL_TILE // LANES
                    lanes = slice((p % per_tile) * LANES, (p % per_tile + 1) * LANES)
                    return ref[p // per_tile, rws, lanes].astype(F32)
                return ref[rws, cols].astype(F32)

            tok_f = [tok(ref, rows_f) for ref in (rf_ref, kf_ref, vf_ref, kkf_ref, bf_ref)]
            tok_b = [tok(ref, rows_b) for ref in (rb_ref, kb_ref, vb_ref, kkb_ref, bb_ref)]
            chains.append((*tok_f, cumf_ref[rows_f, cols], tri_f, False))
            chains.append((*tok_b, cumb_ref[rows_b, cols], tri_b, True))
        yield from _chunk_parallel(chains, common, out)

    def sequential(step, par):
        rows_f, rows_b = rows(step)
        ys = []
        yield from _chunk_sequential(par, states, ys, common)
        for p in range(n_pairs):
            cols = slice(p * LANES, (p + 1) * LANES)
            yf_ref[rows_f, cols] = ys[2 * p].astype(yf_ref.dtype)
            yb_ref[rows_b, cols] = ys[2 * p + 1].astype(yb_ref.dtype)

    states = [state_ref[d, p] for p in range(n_pairs) for d in range(2)]
    pending = None
    for step in range(0, n_chunks, 2):
        par = [{}, {}]
        gens = [parallel(step, par[0]), parallel(step + 1, par[1])]
        if pending is not None:
            gens.insert(0, _chain_gens(sequential(step - 2, pending[0]), sequential(step - 1, pending[1])))
        _interleave(*gens)
        pending = par
    _interleave(_chain_gens(sequential(n_chunks - 2, pending[0]), sequential(n_chunks - 1, pending[1])))
    for p in range(n_pairs):
        for d in range(2):
            state_ref[d, p] = states[2 * p + d]


def _scan(h, k, kk, b, cumf, cumb, seq):
    m = k.shape[0]
    n_chunks = SCAN_CHUNKS
    tb = n_chunks * CHUNK
    nb = seq // tb
    group = RWKV_WIDTH // COL_TILE
    fwd = lambda s, i: (s * nb + i, 0)
    bwd = lambda s, i: (s * nb + nb - 1 - i, 0)
    blk = lambda im: pl.BlockSpec((tb, RWKV_WIDTH), im)
    hblk = lambda im, g: pl.BlockSpec((group, tb, COL_TILE), lambda s, i: (g, im(s, i)[0], 0))
    out = jax.ShapeDtypeStruct((m, RWKV_WIDTH), BF16)
    specs = lambda im: [hblk(im, 0), blk(im), hblk(im, 2), blk(im), blk(im), blk(im)]
    return pl.pallas_call(
        functools.partial(_scan_kernel, n_chunks),
        grid=(m // seq, nb),
        in_specs=specs(fwd) + specs(bwd),
        out_specs=[blk(fwd), blk(bwd)],
        out_shape=[out, out],
        scratch_shapes=[pltpu.VMEM((2, RWKV_WIDTH // LANES, LANES, LANES), F32)],
        compiler_params=_cparams(2),
        name="scan",
    )(h, k, h, kk, b, cumf, h, k, h, kk, b, cumb)


def _post_kernel(seq,
                 yf_ref, yb_ref, g_ref, bonus_ref, cv_ref, cvp_ref, cvn_ref,
                 lng_ref, lnb_ref, cw_ref, x_ref, gt_ref, w_ref, o_ref, mix_ref):
    tm = yf_ref.shape[0]
    t0 = (pl.program_id(0) * tm) % seq
    first = t0 == 0
    last = t0 + tm == seq
    y = yf_ref[...].astype(F32) + yb_ref[...].astype(F32)
    mean = _head_sum_bcast(y) * (1.0 / HEAD)
    d = y - mean
    var = _head_sum_bcast(d * d) * (1.0 / HEAD)
    yn = d * lax.rsqrt(var + LNX_EPS) * lng_ref[...] + lnb_ref[...]
    mix_ref[:, 0:RWKV_WIDTH] = ((yn + bonus_ref[...].astype(F32)) * g_ref[...].astype(F32)).astype(BF16)

    gate_z = lambda ref, rws=slice(None): (_tile_cols(ref, CONV_WIDTH, CONV_WIDTH, rws).astype(F32)
                                           * _tile_cols(ref, 2 * CONV_WIDTH, CONV_WIDTH, rws).astype(F32))
    z = gate_z(cv_ref)
    zp = jnp.where(first, 0.0, gate_z(cvp_ref)[HALO - 1:HALO])
    zn = jnp.where(last, 0.0, gate_z(cvn_ref)[0:1])
    ridx = lax.broadcasted_iota(jnp.int32, z.shape, 0)
    z_prev = jnp.where(ridx == 0, zp, pltpu.roll(z, 1, axis=0))
    z_next = jnp.where(ridx == tm - 1, zn, pltpu.roll(z, tm - 1, axis=0))
    cw = cw_ref[...]
    conv = z_prev * cw[0:1] + z * cw[1:2] + z_next * cw[2:3]
    mix_ref[:, RWKV_WIDTH:] = (_tile_cols(cv_ref, 0, CONV_WIDTH).astype(F32) * conv).astype(BF16)

    gt = gt_ref[0]
    for c0 in range(0, o_ref.shape[1], POST_OUT_COLS):
        cols = slice(c0, c0 + POST_OUT_COLS)
        o_ref[:, cols] = x_ref[:, cols] + gt[:, cols] * _dot(mix_ref[...], w_ref[:, cols])


def _post(yf, yb, g, bonus, h, lnx_g, lnx_b, conv_w, x2, gt1, w_out, seq):
    m, d = x2.shape
    tm = ELEMENTWISE_ROWS
    per_seq = seq // tm
    n_halo = m // HALO
    n_cv = h.shape[0]
    row = lambda i: (i, 0)
    const = lambda i: (0, 0)
    return pl.pallas_call(
        functools.partial(_post_kernel, seq),
        grid=(m // tm,),
        in_specs=[pl.BlockSpec((tm, RWKV_WIDTH), row)] * 4 + [
            pl.BlockSpec((n_cv, tm, COL_TILE), lambda i: (0, i, 0)),
            pl.BlockSpec((n_cv, HALO, COL_TILE), lambda i: (0, jnp.maximum(i * (tm // HALO) - 1, 0), 0)),
            pl.BlockSpec((n_cv, HALO, COL_TILE),
                         lambda i: (0, jnp.minimum((i + 1) * (tm // HALO), n_halo - 1), 0)),
            pl.BlockSpec((1, RWKV_WIDTH), const),
            pl.BlockSpec((1, RWKV_WIDTH), const),
            pl.BlockSpec((3, CONV_WIDTH), const),
            pl.BlockSpec((tm, d), row),
            pl.BlockSpec((1, 1, d), lambda i: (i // per_seq, 0, 0)),
            pl.BlockSpec((d, d), const, pipeline_mode=pl.Buffered(1))],
        out_specs=pl.BlockSpec((tm, d), row),
        out_shape=jax.ShapeDtypeStruct((m, d), F32),
        scratch_shapes=[pltpu.VMEM((tm, d), BF16)],
        compiler_params=_cparams(1),
        name="post",
    )(yf, yb, g, bonus, h, h, h, lnx_g, lnx_b, conv_w, x2, gt1, w_out)


def _ffn_kernel(n_split, x_ref, sc_ref, sh_ref, gt_ref, g2_ref, gf_ref, wg_ref, wu_ref, wd_ref, o_ref, hn_ref):
    j = pl.program_id(1)
    last = pl.num_programs(1) - 1
    tm, n_out = o_ref.shape
    rows = tm // n_split

    def swiglu_step(sl, init):
        hn = hn_ref[sl, :]
        gate = _dot(hn, wg_ref[...])
        up = _dot(hn, wu_ref[...])
        act = (gate * jax.nn.sigmoid(gate) * up).astype(BF16)
        for c0 in range(0, n_out, FFN_DOWN_COLS):
            cols = slice(c0, c0 + FFN_DOWN_COLS)
            part = _dot(act, wd_ref[:, cols])
            if init:
                o_ref[sl, cols] = part
            else:
                o_ref[sl, cols] += part

    @pl.when(j == 0)
    def _():
        for q in range(n_split):
            sl = slice(q * rows, (q + 1) * rows)
            x = x_ref[sl, :]
            ms = jnp.mean(x * x, axis=-1, keepdims=True)
            y = x * lax.rsqrt(ms + RMS_EPS) * g2_ref[...]
            hn_ref[sl, :] = (y * (1.0 + sc_ref[0]) + sh_ref[0]).astype(BF16)
            swiglu_step(sl, True)

    @pl.when((j > 0) & (j < last))
    def _():
        swiglu_step(slice(0, tm), False)

    @pl.when(j == last)
    def _():
        for q in range(n_split):
            sl = slice(q * rows, (q + 1) * rows)
            swiglu_step(sl, False)
            x = x_ref[sl, :] + gt_ref[0] * o_ref[sl, :]
            ms = jnp.mean(x * x, axis=-1, keepdims=True)
            o_ref[sl, :] = x * lax.rsqrt(ms + RMS_EPS) * gf_ref[...]


def _ffn(x1, sc, sh, gt, g2, gf, wg, wu, wd, seq):
    m, d = x1.shape
    tm, tf, n_split = MATMUL_ROWS, FFN_TILE, NORM_SPLIT
    n_ff = wg.shape[1] // tf
    per_seq = seq // tm
    row = lambda i, j: (i, 0)
    mod = lambda i, j: (i // per_seq, 0, 0)
    const = lambda i, j: (0, 0)
    return pl.pallas_call(
        functools.partial(_ffn_kernel, n_split),
        grid=(m // tm, n_ff),
        in_specs=[pl.BlockSpec((tm, d), row),
                  pl.BlockSpec((1, 1, d), mod),
                  pl.BlockSpec((1, 1, d), mod),
                  pl.BlockSpec((1, 1, d), mod),
                  pl.BlockSpec((1, d), const),
                  pl.BlockSpec((1, d), const),
                  pl.BlockSpec((d, tf), lambda i, j: (0, j)),
                  pl.BlockSpec((d, tf), lambda i, j: (0, j)),
                  pl.BlockSpec((tf, d), lambda i, j: (j, 0))],
        out_specs=pl.BlockSpec((tm, d), row),
        out_shape=jax.ShapeDtypeStruct((m, d), F32),
        scratch_shapes=[pltpu.VMEM((tm, d), BF16)],
        compiler_params=_cparams(2),
        name="ffn",
    )(x1, sc, sh, gt, g2, gf, wg, wu, wd)


def _layer_weights(w_in, mu_shift, w2_decay, a2, g2_gate):
    d = w_in.shape[0]
    n_rw = RKV_COLS + SMALL_COLS
    pad = RW_COLS - n_rw
    tile_major = lambda w: w.reshape(d, -1, COL_TILE).transpose(1, 0, 2)
    w_rw = tile_major(jnp.pad(w_in[:, :n_rw].astype(BF16), ((0, 0), (0, pad))))
    w_cv = tile_major(w_in[:, n_rw:].astype(BF16))
    mu_rw = jnp.pad(mu_shift, (0, pad)).reshape(1, RW_COLS)
    zeros = lambda rows, cols: jnp.zeros((rows, cols), F32)
    wfb = jnp.concatenate([
        jnp.concatenate([w2_decay[0], zeros(DECAY_RANK, RWKV_WIDTH)], axis=1),
        jnp.concatenate([zeros(DECAY_RANK, RWKV_WIDTH), w2_decay[1]], axis=1)], axis=0).astype(BF16)
    wa = jnp.concatenate([a2, zeros(LANES - AAA_RANK, RWKV_WIDTH)], axis=0).astype(BF16)
    g_rows = SMALL_PAD - LANES
    wg = jnp.concatenate([zeros(AAA_RANK, RWKV_WIDTH), g2_gate,
                          zeros(g_rows - AAA_RANK - GATE_RANK, RWKV_WIDTH)], axis=0).astype(BF16)
    return w_rw, w_cv, mu_rw, wfb, wa, wg


def _group(x, mod, lw):
    nb, seq, d = x.shape
    x2 = x.reshape(nb * seq, d)
    sh1, sc1, gt1, sh2, sc2, gt2 = [mod[:, q * d:(q + 1) * d].reshape(nb, 1, d) for q in range(6)]
    h, h_cv = _in_proj(x2, sc1, sh1, lw["norm1_g"], lw["w_rw"], lw["w_cv"], lw["mu_rw"], seq)
    k, kk, b, cumf, cumb, g, bonus = _prep(
        h, lw["wfb"], lw["wa"], lw["wg"], lw["w0"], lw["a0"], lw["k_k"], lw["k_a"], lw["r_k"])
    yf, yb = _scan(h, k, kk, b, cumf, cumb, seq)
    x1 = _post(yf, yb, g, bonus, h_cv, lw["lnx_g"], lw["lnx_b"], lw["conv_w"], x2, gt1, lw["w_out"], seq)
    out = _ffn(x1, sc2, sh2, gt2, lw["norm2_g"], lw["norm_f_g"], lw["wg_ffn"], lw["wu_ffn"], lw["wd_ffn"], seq)
    return out.reshape(nb, seq, d)


def kernel(x_prompt, x_sample, c_prompt, c_sample, w_ada, b_ada, norm1_g, w_in, mu_shift, w0_decay, w2_decay, a0, a2, g2_gate, k_k, k_a, r_k, lnx_g, lnx_b, conv_w, w_out, norm2_g, w_ffn_gate, w_ffn_up, w_ffn_down, norm_f_g):
    assert w_ada.shape[0] == 1, "single-layer block"
    l = 0
    row = lambda t: t.reshape(1, -1)
    w_rw, w_cv, mu_rw, wfb, wa, wg = _layer_weights(w_in[l], mu_shift[l], w2_decay[l], a2[l], g2_gate[l])
    lw = dict(w_rw=w_rw, w_cv=w_cv, mu_rw=mu_rw, wfb=wfb, wa=wa, wg=wg,
              norm1_g=row(norm1_g[l]), w0=w0_decay[l], a0=row(a0[l]), k_k=row(k_k[l]), k_a=row(k_a[l]),
              r_k=row(r_k[l]), lnx_g=row(lnx_g[l]), lnx_b=row(lnx_b[l]), conv_w=conv_w[l],
              w_out=w_out[l].astype(BF16), norm2_g=row(norm2_g[l]), norm_f_g=row(norm_f_g),
              wg_ffn=w_ffn_gate[l].astype(BF16), wu_ffn=w_ffn_up[l].astype(BF16),
              wd_ffn=w_ffn_down[l].astype(BF16))
    n_prompt = c_prompt.shape[0]
    mod = _mod(jnp.concatenate([c_prompt, c_sample], axis=0), w_ada[l], b_ada[l])
    y_prompt = _group(x_prompt, mod[:n_prompt], lw)
    y_sample = _group(x_sample, mod[n_prompt:], lw)
    return (y_prompt, y_sample)
```
